```python
import math
import jax, jax.numpy as jnp
from jax import lax
import numpy as np

D_MODEL = 1024
BATCH = 8
SEQ = 2048
DEPTH = 4
DEC_BATCH = 128
DEC_SEQ = 8
PAST_LEN = 16384
PAGE_SIZE = 128

H_A = 4
DK_A = 128
DV_A = 256
W_A_QK = H_A * DK_A
W_A_V = H_A * DV_A
H_B = 8
DK_B = 128
DV_B = 128
W_B_QK = H_B * DK_B
W_B_V = H_B * DV_B
CONV_W = 4
C_CONV = 2 * W_B_QK + W_B_V
D_FF = 4 * D_MODEL
PLE_DIM = 256
MLSTM_CHUNK = 64
GDN_CHUNK = 64
EPS = 1e-6
F_BIAS_LO = 3.0
F_BIAS_HI = 6.0
IN_SIZES = (W_A_QK, W_A_QK, W_A_V, H_A, H_A, W_A_V, C_CONV, H_B, H_B, W_B_V, D_MODEL, D_MODEL)
N_IN = sum(IN_SIZES)

kernel_name = 'hybrid_mlstm_gdn_decode_step'


def _rmsnorm(x, g):
    xf = x.astype(jnp.float32)
    y = xf * lax.rsqrt(jnp.mean(xf * xf, axis=-1, keepdims=True) + EPS)
    return (y * g.astype(jnp.float32)).astype(x.dtype)


def _head_rms(t):
    return t * lax.rsqrt(jnp.mean(t * t, axis=-1, keepdims=True) + EPS)


def _l2norm(t):
    return t * lax.rsqrt(jnp.sum(t * t, axis=-1, keepdims=True) + EPS)


def _to_chunks(t, L):
    B, T, H = t.shape[:3]
    rest = t.shape[3:]
    t = t.reshape((B, T // L, L, H) + rest)
    return t.transpose((1, 0, 3, 2) + tuple(range(4, t.ndim)))


def _from_chunks(t):
    NC, B, H, L, E = t.shape
    return t.transpose(1, 0, 3, 2, 4).reshape(B, NC * L, H, E)


def _mlstm(q, k, v, i_pre, f_pre, C0, n0, m0):
    B, T, H, DK = q.shape
    L = math.gcd(T, MLSTM_CHUNK)
    q = q * (DK ** -0.5)
    qc, kc, vc = _to_chunks(q, L), _to_chunks(k, L), _to_chunks(v, L)
    ic = _to_chunks(i_pre, L)
    lfc = _to_chunks(jax.nn.log_sigmoid(f_pre), L)
    idx = jnp.arange(L)
    causal = idx[:, None] >= idx[None, :]

    def step(carry, inp):
        C, n, m = carry
        qx, kx, vx, ix, lfx = inp
        b = jnp.cumsum(lfx, axis=-1)
        dlog = jnp.where(causal, b[..., :, None] - b[..., None, :] + ix[..., None, :], -jnp.inf)
        inter = b + m[..., None]
        m_row = jnp.maximum(inter, jnp.max(dlog, axis=-1))
        dw = jnp.exp(dlog - m_row[..., None])
        w_inter = jnp.exp(inter - m_row)
        s = jnp.einsum('bhld,bhjd->bhlj', qx, kx) * dw
        num = w_inter[..., None] * jnp.einsum('bhld,bhde->bhle', qx, C) + jnp.einsum('bhlj,bhje->bhle', s, vx)
        nq = w_inter * jnp.einsum('bhld,bhd->bhl', qx, n) + jnp.sum(s, axis=-1)
        den = jnp.maximum(jnp.abs(nq), jnp.exp(-m_row))
        h = num / den[..., None]
        m_new = m_row[..., -1]
        dec = jnp.exp(b[..., -1] + m - m_new)
        wk = jnp.exp(ix + b[..., -1:] - b - m_new[..., None])
        C_new = dec[..., None, None] * C + jnp.einsum('bhl,bhld,bhle->bhde', wk, kx, vx)
        n_new = dec[..., None] * n + jnp.einsum('bhl,bhld->bhd', wk, kx)
        return (C_new, n_new, m_new), h

    (C, n, m), h = lax.scan(step, (C0, n0, m0), (qc, kc, vc, ic, lfc))
    return _from_chunks(h), C, n, m


def _gated_delta(q, k, v, g, beta, S0):
    B, T, H, DK = q.shape
    DV = v.shape[-1]
    L = math.gcd(T, GDN_CHUNK)
    q = q * (DK ** -0.5)
    qc, kc, vc = _to_chunks(q, L), _to_chunks(k, L), _to_chunks(v, L)
    gc, bc = _to_chunks(g, L), _to_chunks(beta, L)
    gam = jnp.cumsum(gc, axis=-1)
    idx = jnp.arange(L)
    lower = idx[:, None] >= idx[None, :]
    strict = idx[:, None] > idx[None, :]
    decay = jnp.exp(jnp.where(lower, gam[..., :, None] - gam[..., None, :], -jnp.inf))
    kb = kc * bc[..., None]
    m_kk = jnp.where(strict, jnp.einsum('nbhid,nbhjd->nbhij', kb, kc) * decay, 0.0)
    a_mat = m_kk + jnp.eye(L, dtype=m_kk.dtype)
    rhs = jnp.concatenate([vc * bc[..., None], kb * jnp.exp(gam)[..., None]], axis=-1)
    sol = lax.linalg.triangular_solve(a_mat, rhs, left_side=True, lower=True, unit_diagonal=True)
    w_v, w_k = sol[..., :DV], sol[..., DV:]
    qk = jnp.einsum('nbhid,nbhjd->nbhij', qc, kc) * decay
    q_dec = qc * jnp.exp(gam)[..., None]
    k_dec = kc * jnp.exp(gam[..., -1:] - gam)[..., None]
    g_last = jnp.exp(gam[..., -1])

    def step(S, inp):
        wv, wk, qkx, qd, kd, gl = inp
        u = wv - jnp.einsum('bhld,bhde->bhle', wk, S)
        o = jnp.einsum('bhld,bhde->bhle', qd, S) + jnp.einsum('bhlj,bhje->bhle', qkx, u)
        S = S * gl[..., None, None] + jnp.einsum('bhld,bhle->bhde', kd, u)
        return S, o

    S, o = lax.scan(step, S0, (w_v, w_k, qk, q_dec, k_dec, g_last))
    return _from_chunks(o), S


def _causal_conv(x, buf, w):
    T = x.shape[1]
    xp = jnp.concatenate([buf.astype(x.dtype), x], axis=1)
    out = xp[:, 0:T] * w[0]
    for j in range(1, CONV_W):
        out = out + xp[:, j:j + T] * w[j]
    return out, xp[:, -(CONV_W - 1):]


def _layer(x, p, C, n, m, S, conv_buf, g_mix, w_in, b_igate, b_fgate, g_norm_a, conv_w, a_log, dt_bias,
           g_norm_b, w_pa, w_pb, w_out, g_ffn, w_up, w_down, g_ple, w_pg, w_pp):
    f32 = jnp.float32
    B, T, _ = x.shape
    h = _rmsnorm(x, g_mix)
    proj = h @ w_in
    split_idx = np.cumsum(IN_SIZES)[:-1].tolist()
    (q_a, k_a, v_a, i_pre, f_pre, o_a, qkv_b, a_b, beta_b, z_b, gate_a, gate_b) = jnp.split(proj, split_idx, axis=-1)

    ha, C, n, m = _mlstm(q_a.reshape(B, T, H_A, DK_A).astype(f32), k_a.reshape(B, T, H_A, DK_A).astype(f32),
                         v_a.reshape(B, T, H_A, DV_A).astype(f32), i_pre.astype(f32) + b_igate.astype(f32),
                         f_pre.astype(f32) + b_fgate.astype(f32), C.astype(f32), n.astype(f32), m.astype(f32))
    ha = _head_rms(ha).reshape(B, T, W_A_V) * g_norm_a.astype(f32) * jax.nn.sigmoid(o_a.astype(f32))

    cv, conv_buf = _causal_conv(qkv_b, conv_buf, conv_w)
    cv = jax.nn.silu(cv.astype(f32))
    q_b = _l2norm(cv[..., :W_B_QK].reshape(B, T, H_B, DK_B))
    k_b = _l2norm(cv[..., W_B_QK:2 * W_B_QK].reshape(B, T, H_B, DK_B))
    v_b = cv[..., 2 * W_B_QK:].reshape(B, T, H_B, DV_B)
    g = -jnp.exp(a_log.astype(f32)) * jax.nn.softplus(a_b.astype(f32) + dt_bias.astype(f32))
    beta = jax.nn.sigmoid(beta_b.astype(f32))
    hb, S = _gated_delta(q_b, k_b, v_b, g, beta, S.astype(f32))
    hb = _head_rms(hb) * g_norm_b.astype(f32) * jax.nn.silu(z_b.astype(f32).reshape(B, T, H_B, DV_B))
    hb = hb.reshape(B, T, W_B_V)

    mix = jax.nn.sigmoid(gate_a) * (ha.astype(x.dtype) @ w_pa) + jax.nn.sigmoid(gate_b) * (hb.astype(x.dtype) @ w_pb)
    x = x + mix @ w_out
    u = _rmsnorm(x, g_ffn) @ w_up
    x = x + jnp.square(jax.nn.relu(u)) @ w_down
    x = x + jax.nn.sigmoid(_rmsnorm(x, g_ple) @ w_pg) * (p @ w_pp)
    return x, C, n, m, S, conv_buf


def setup_inputs(seed: int = 0) -> dict:
    key = jax.random.key(seed)
    ks = jax.random.split(key, 32)
    f32 = jnp.float32

    def nrm(k, shape, s):
        return (s * jax.random.normal(k, shape, dtype=f32)).astype(f32)

    dt = jnp.exp(jax.random.uniform(ks[16], (DEPTH, H_B), minval=math.log(1e-3), maxval=math.log(1e-1), dtype=f32))
    return {
        'x_prompt': nrm(ks[0], (BATCH, SEQ, D_MODEL), 1.0),
        'x_sample': nrm(ks[1], (DEC_BATCH, DEC_SEQ, D_MODEL), 1.0),
        'p_prompt': nrm(ks[2], (DEPTH, BATCH, SEQ, PLE_DIM), 1.0),
        'p_sample': nrm(ks[3], (DEPTH, DEC_BATCH, DEC_SEQ, PLE_DIM), 1.0),
        'state_mlstm_C': nrm(ks[4], (DEPTH, DEC_BATCH, H_A, DK_A, DV_A), 0.5),
        'state_mlstm_n': nrm(ks[5], (DEPTH, DEC_BATCH, H_A, DK_A), 0.5),
        'state_mlstm_m': 1.0 + nrm(ks[6], (DEPTH, DEC_BATCH, H_A), 0.5),
        'state_gdn_S': nrm(ks[7], (DEPTH, DEC_BATCH, H_B, DK_B, DV_B), 0.3),
        'state_gdn_conv': nrm(ks[8], (DEPTH, DEC_BATCH, CONV_W - 1, C_CONV), 1.0),
        'g_mix': 1.0 + nrm(ks[9], (DEPTH, D_MODEL), 0.05),
        'w_in': nrm(ks[10], (DEPTH, D_MODEL, N_IN), D_MODEL ** -0.5),
        'b_igate': nrm(ks[11], (DEPTH, H_A), 0.1),
        'b_fgate': jnp.linspace(F_BIAS_LO, F_BIAS_HI, H_A, dtype=f32)[None, :] + nrm(ks[12], (DEPTH, H_A), 0.1),
        'g_norm_a': 1.0 + nrm(ks[13], (DEPTH, W_A_V), 0.05),
        'conv_w': nrm(ks[14], (DEPTH, CONV_W, C_CONV), CONV_W ** -0.5),
        'a_log': jnp.log(jax.random.uniform(ks[15], (DEPTH, H_B), minval=1.0, maxval=16.0, dtype=f32)),
        'dt_bias': dt + jnp.log(-jnp.expm1(-dt)),
        'g_norm_b': 1.0 + nrm(ks[17], (DEPTH, DV_B), 0.05),
        'w_pa': nrm(ks[18], (DEPTH, W_A_V, D_MODEL), W_A_V ** -0.5),
        'w_pb': nrm(ks[19], (DEPTH, W_B_V, D_MODEL), W_B_V ** -0.5),
        'w_out': nrm(ks[20], (DEPTH, D_MODEL, D_MODEL), D_MODEL ** -0.5),
        'g_ffn': 1.0 + nrm(ks[21], (DEPTH, D_MODEL), 0.05),
        'w_up': nrm(ks[22], (DEPTH, D_MODEL, D_FF), D_MODEL ** -0.5),
        'w_down': nrm(ks[23], (DEPTH, D_FF, D_MODEL), D_FF ** -0.5),
        'g_ple': 1.0 + nrm(ks[24], (DEPTH, D_MODEL), 0.05),
        'w_pg': nrm(ks[25], (DEPTH, D_MODEL, D_MODEL), D_MODEL ** -0.5),
        'w_pp': nrm(ks[26], (DEPTH, PLE_DIM, D_MODEL), PLE_DIM ** -0.5),
        'g_final': 1.0 + nrm(ks[27], (D_MODEL,), 0.05),
    }


def reference(x_prompt, x_sample, p_prompt, p_sample, state_mlstm_C, state_mlstm_n, state_mlstm_m,
              state_gdn_S, state_gdn_conv, g_mix, w_in, b_igate, b_fgate, g_norm_a, conv_w, a_log, dt_bias,
              g_norm_b, w_pa, w_pb, w_out, g_ffn, w_up, w_down, g_ple, w_pg, w_pp, g_final):
    f32 = jnp.float32
    bp = x_prompt.shape[0]
    zC = jnp.zeros((bp, H_A, DK_A, DV_A), f32)
    zn = jnp.zeros((bp, H_A, DK_A), f32)
    zm = jnp.zeros((bp, H_A), f32)
    zS = jnp.zeros((bp, H_B, DK_B, DV_B), f32)
    zconv = jnp.zeros((bp, CONV_W - 1, C_CONV), x_prompt.dtype)
    xp, xs = x_prompt, x_sample
    pC, pn, pm, pS, pc = [], [], [], [], []
    sC, sn, sm, sS, sc = [], [], [], [], []
    for i in range(DEPTH):
        lw = (g_mix[i], w_in[i], b_igate[i], b_fgate[i], g_norm_a[i], conv_w[i], a_log[i], dt_bias[i],
              g_norm_b[i], w_pa[i], w_pb[i], w_out[i], g_ffn[i], w_up[i], w_down[i], g_ple[i], w_pg[i], w_pp[i])
        xp, c1, n1, m1, s1, v1 = _layer(xp, p_prompt[i], zC, zn, zm, zS, zconv, *lw)
        xs, c2, n2, m2, s2, v2 = _layer(xs, p_sample[i], state_mlstm_C[i], state_mlstm_n[i], state_mlstm_m[i],
                                        state_gdn_S[i], state_gdn_conv[i], *lw)
        pC.append(c1); pn.append(n1); pm.append(m1); pS.append(s1); pc.append(v1)
        sC.append(c2); sn.append(n2); sm.append(m2); sS.append(s2); sc.append(v2)
    y_prompt = _rmsnorm(xp, g_final)
    y_sample = _rmsnorm(xs, g_final)
    return (y_prompt, y_sample,
            jnp.stack(pC), jnp.stack(pn), jnp.stack(pm), jnp.stack(pS), jnp.stack(pc),
            jnp.stack(sC), jnp.stack(sn), jnp.stack(sm), jnp.stack(sS), jnp.stack(sc))
```

```python
import functools
import math

import jax
import jax.numpy as jnp
from jax import lax
from jax.experimental import pallas as pl
from jax.experimental.pallas import tpu as pltpu

F32 = jnp.float32
BF16 = jnp.bfloat16

D_MODEL = 1024
H_A, DK_A, DV_A = 4, 128, 256
H_B, DK_B, DV_B = 8, 128, 128
W_A_QK, W_A_V = H_A * DK_A, H_A * DV_A
W_B_QK, W_B_V = H_B * DK_B, H_B * DV_B
CONV_W = 4
C_CONV = 2 * W_B_QK + W_B_V
D_FF = 4 * D_MODEL
PLE_DIM = 256
CHUNK = 64
EPS = 1e-6

LANES = 128
SUBLANES = 8

COL_QA, COL_KA, COL_VA, COL_OA = 0, 512, 1024, 2048
COL_QKVB, COL_ZB, COL_GA, COL_GB = 3072, 6144, 7168, 8192
N_MAIN = 9216
SM_I, SM_F, SM_A, SM_BETA = 0, 4, 8, 16

VMEM_LIMIT = 56 * 1024 * 1024


def _bf(x):
    return x.astype(BF16)


def _dot(a, b):
    return jnp.dot(_bf(a), _bf(b), preferred_element_type=F32)


def _dot_nt(a, b):
    return lax.dot_general(_bf(a), _bf(b), (((1,), (1,)), ((), ())), preferred_element_type=F32)


def _dot_tn(a, b):
    return lax.dot_general(_bf(a), _bf(b), (((0,), (0,)), ((), ())), preferred_element_type=F32)


def _split2(x):
    hi = _bf(x)
    lo = _bf(x - hi.astype(F32))
    return hi, lo


def _split3(x):
    hi = _bf(x)
    r = x - hi.astype(F32)
    mid = _bf(r)
    lo = _bf(r - mid.astype(F32))
    return hi, mid, lo


def _dot_sel(sel_bf, x):
    hi, mid, lo = _split3(x)
    d = lambda p: jnp.dot(sel_bf, p, preferred_element_type=F32)
    return d(hi) + d(mid) + d(lo)


def _dot_sel_nt(sel_bf, x):
    hi, mid, lo = _split3(x)
    d = lambda p: lax.dot_general(sel_bf, p, (((1,), (1,)), ((), ())), preferred_element_type=F32)
    return d(hi) + d(mid) + d(lo)


def _dot_x3(a, b):
    ah, al = _split2(a)
    bh, bl = _split2(b)
    d = lambda p, q: jnp.dot(p, q, preferred_element_type=F32)
    return d(ah, bh) + d(ah, bl) + d(al, bh)


def _sigmoid(x):
    return 1.0 / (1.0 + jnp.exp(-x))


def _softplus(x):
    return jnp.maximum(x, 0.0) + jnp.log1p(jnp.exp(-jnp.abs(x)))


def _rmsnorm(x, g):
    return x * lax.rsqrt(jnp.mean(x * x, axis=-1, keepdims=True) + EPS) * g


def _iota(shape, dim):
    return lax.broadcasted_iota(jnp.int32, shape, dim)


def _seg_masks(n, seg):
    row, col = _iota((n, n), 0), _iota((n, n), 1)
    if seg == n:
        lower = col <= row
        strict = col < row
    else:
        same = (row // seg) == (col // seg)
        lower = jnp.logical_and(col <= row, same)
        strict = jnp.logical_and(col < row, same)
    return lower, strict


def _one_hot_bf(mask):
    return jnp.where(mask, 1.0, 0.0).astype(BF16)


def _in_kernel(x_ref, g_ref, wm_ref, ws_ref, om_ref, os_ref, h_ref):
    @pl.when(pl.program_id(1) == 0)
    def _():
        h_ref[...] = _bf(_rmsnorm(x_ref[...], g_ref[...]))
        os_ref[...] = jnp.dot(h_ref[...], ws_ref[...], preferred_element_type=F32)

    om_ref[...] = jnp.dot(h_ref[...], wm_ref[...], preferred_element_type=F32)


def _in_proj(x, g, w_main, w_small):
    m = x.shape[0]
    tm = min(m, 1024)
    tn = 1024
    return pl.pallas_call(
        _in_kernel,
        grid=(m // tm, N_MAIN // tn),
        in_specs=[
            pl.BlockSpec((tm, D_MODEL), lambda i, j: (i, 0)),
            pl.BlockSpec((1, D_MODEL), lambda i, j: (0, 0)),
            pl.BlockSpec((D_MODEL, tn), lambda i, j: (0, j)),
            pl.BlockSpec((D_MODEL, LANES), lambda i, j: (0, 0)),
        ],
        out_specs=[
            pl.BlockSpec((tm, tn), lambda i, j: (i, j)),
            pl.BlockSpec((tm, LANES), lambda i, j: (i, 0)),
        ],
        out_shape=[jax.ShapeDtypeStruct((m, N_MAIN), F32), jax.ShapeDtypeStruct((m, LANES), F32)],
        scratch_shapes=[pltpu.VMEM((tm, D_MODEL), BF16)],
        compiler_params=pltpu.CompilerParams(
            dimension_semantics=("parallel", "arbitrary"), vmem_limit_bytes=VMEM_LIMIT),
        name="in_proj",
    )(x, g, w_main, w_small)


def _post_kernel(x_ref, ha_ref, hb_ref, ga_ref, gb_ref, p_ref, wpa_ref, wpb_ref, wout_ref, gffn_ref,
                 wup_ref, wdown_ref, gple_ref, wpg_ref, wpp_ref, gfin_ref, o_ref, *, final):
    x = x_ref[...]
    a = jnp.dot(_bf(ha_ref[...]), wpa_ref[...], preferred_element_type=F32)
    b = jnp.dot(_bf(hb_ref[...]), wpb_ref[...], preferred_element_type=F32)
    mix = _sigmoid(ga_ref[...]) * a + _sigmoid(gb_ref[...]) * b
    x = x + jnp.dot(_bf(mix), wout_ref[...], preferred_element_type=F32)
    hn = _bf(_rmsnorm(x, gffn_ref[...]))
    acc = jnp.zeros_like(x)
    fc = 1024
    for c in range(D_FF // fc):
        u = jnp.dot(hn, wup_ref[:, c * fc:(c + 1) * fc], preferred_element_type=F32)
        r = jnp.maximum(u, 0.0)
        acc = acc + jnp.dot(_bf(r * r), wdown_ref[c * fc:(c + 1) * fc, :], preferred_element_type=F32)
    x = x + acc
    hp = _bf(_rmsnorm(x, gple_ref[...]))
    gate = _sigmoid(jnp.dot(hp, wpg_ref[...], preferred_element_type=F32))
    x = x + gate * jnp.dot(_bf(p_ref[...]), wpp_ref[...], preferred_element_type=F32)
    if final:
        x = _rmsnorm(x, gfin_ref[...])
    o_ref[...] = x


def _post(x, ha, hb, proj_main, p, w, final):
    m = x.shape[0]
    tm = min(m, 256)
    row = lambda i: (i, 0)
    const = lambda i: (0, 0)
    wspec = lambda shape: pl.BlockSpec(shape, const, pipeline_mode=pl.Buffered(1))
    return pl.pallas_call(
        functools.partial(_post_kernel, final=final),
        grid=(m // tm,),
        in_specs=[
            pl.BlockSpec((tm, D_MODEL), row),
            pl.BlockSpec((tm, W_A_V), row),
            pl.BlockSpec((tm, W_B_V), row),
            pl.BlockSpec((tm, D_MODEL), lambda i: (i, COL_GA // D_MODEL)),
            pl.BlockSpec((tm, D_MODEL), lambda i: (i, COL_GB // D_MODEL)),
            pl.BlockSpec((tm, PLE_DIM), row),
            wspec((W_A_V, D_MODEL)),
            wspec((W_B_V, D_MODEL)),
            wspec((D_MODEL, D_MODEL)),
            wspec((1, D_MODEL)),
            wspec((D_MODEL, D_FF)),
            wspec((D_FF, D_MODEL)),
            wspec((1, D_MODEL)),
            wspec((D_MODEL, D_MODEL)),
            wspec((PLE_DIM, D_MODEL)),
            wspec((1, D_MODEL)),
        ],
        out_specs=pl.BlockSpec((tm, D_MODEL), row),
        out_shape=jax.ShapeDtypeStruct((m, D_MODEL), F32),
        compiler_params=pltpu.CompilerParams(
            dimension_semantics=("parallel",), vmem_limit_bytes=VMEM_LIMIT),
        name="post_final" if final else "post",
    )(x, ha, hb, proj_main, proj_main, p, w["w_pa"], w["w_pb"], w["w_out"], w["g_ffn"], w["w_up"],
      w["w_down"], w["g_ple"], w["w_pg"], w["w_pp"], w["g_final"])


def _gate_tables(small, prm, kind, lower_bf, eye_bf):
    n = small.shape[0]
    lane = _iota((n, LANES), 1)
    p = small + prm[0:1, :]
    if kind == "mlstm":
        lf = -_softplus(-p)
        pm = jnp.where(lane < SM_F, p, jnp.where(lane < SM_A, lf, 0.0))
    else:
        g = -jnp.exp(prm[1:2, :]) * _softplus(p)
        beta = _sigmoid(p)
        in_a = jnp.logical_and(lane >= SM_A, lane < SM_BETA)
        in_b = jnp.logical_and(lane >= SM_BETA, lane < SM_BETA + H_B)
        pm = jnp.where(in_a, g, jnp.where(in_b, beta, 0.0))
    cs = _dot_sel(lower_bf, pm)
    pm_t = _dot_sel_nt(eye_bf, pm)
    cs_t = _dot_sel_nt(eye_bf, cs)
    return pm, cs, pm_t, cs_t


def _mlstm_intra(qh, kh, i_row, i_lane, b_row, b_lane, m_rows, lower):
    dlog = jnp.where(lower, b_row - b_lane + i_lane, -jnp.inf)
    inter = b_row + m_rows
    m_row = jnp.maximum(inter, jnp.max(dlog, axis=1, keepdims=True))
    dw = jnp.exp(dlog - m_row)
    w_inter = jnp.exp(inter - m_row)
    s = _dot_nt(qh, kh) * dw
    return s, w_inter, m_row


def _mlstm_out(num, nq, m_row, gna, o):
    den = jnp.maximum(jnp.abs(nq), jnp.exp(-m_row))
    hh = num / den
    hr = hh * lax.rsqrt(jnp.mean(hh * hh, axis=1, keepdims=True) + EPS)
    return hr * gna * _sigmoid(o)


def _mlstm_prompt_kernel(q_ref, k_ref, v_ref, o_ref, sm_ref, prm_ref, gna_ref,
                         ha_ref, c_out, n_out, m_out, c_s, n_s, m_s, *, n_chunks):
    t = pl.program_id(1)
    L = CHUNK

    @pl.when(t == 0)
    def _():
        c_s[...] = jnp.zeros_like(c_s)
        n_s[...] = jnp.zeros_like(n_s)
        m_s[...] = jnp.zeros_like(m_s)

    lower, _ = _seg_masks(L, L)
    lower_bf = _one_hot_bf(lower)
    eye_bf = _one_hot_bf(_iota((LANES, LANES), 0) == _iota((LANES, LANES), 1))
    prm = prm_ref[...]
    scale = DK_A ** -0.5

    def chunk(c, carry):
        rows = pl.ds(pl.multiple_of(c * L, L), L)
        pm, cs, pm_t, cs_t = _gate_tables(sm_ref[rows, :], prm, "mlstm", lower_bf, eye_bf)
        for h in range(H_A):
            qh = q_ref[rows, h * DK_A:(h + 1) * DK_A] * scale
            kh = k_ref[rows, h * DK_A:(h + 1) * DK_A]
            vh = v_ref[rows, h * DV_A:(h + 1) * DV_A]
            i_row = pm[:, SM_I + h:SM_I + h + 1]
            b_row = cs[:, SM_F + h:SM_F + h + 1]
            i_lane = pm_t[SM_I + h:SM_I + h + 1, :]
            b_lane = cs_t[SM_F + h:SM_F + h + 1, :]
            m_prev = m_s[h:h + 1, 0:1]
            s, w_inter, m_row = _mlstm_intra(qh, kh, i_row, i_lane, b_row, b_lane, m_prev, lower)
            ch = c_s[h]
            nh = n_s[h:h + 1, :]
            num = w_inter * _dot(qh, ch) + _dot(s, vh)
            nq = w_inter * jnp.sum(qh * nh, axis=1, keepdims=True) + jnp.sum(s, axis=1, keepdims=True)
            ha_ref[rows, h * DV_A:(h + 1) * DV_A] = _mlstm_out(
                num, nq, m_row, gna_ref[:, h * DV_A:(h + 1) * DV_A], o_ref[rows, h * DV_A:(h + 1) * DV_A])
            m_new = m_row[L - 1:L, :]
            b_last = b_row[L - 1:L, :]
            dec = jnp.exp(b_last + m_prev - m_new)
            kw = kh * jnp.exp(i_row + b_last - b_row - m_new)
            c_s[h] = dec * ch + _dot_tn(kw, vh)
            n_s[h:h + 1, :] = dec * nh + jnp.sum(kw, axis=0, keepdims=True)
            m_s[h:h + 1, :] = jnp.broadcast_to(m_new, (1, LANES))
        return carry

    lax.fori_loop(0, n_chunks, chunk, 0)

    @pl.when(t == pl.num_programs(1) - 1)
    def _():
        c_out[0] = c_s[...]
        n_out[0] = n_s[0:H_A, :]
        m_out[0] = m_s[...]


def _mlstm_prompt(proj_main, proj_small, prm, gna, batch, seq):
    tt = 256
    nt = seq // tt
    rb = lambda col: (lambda b, t: (b * nt + t, col))
    const = lambda b, t: (0, 0)
    return pl.pallas_call(
        functools.partial(_mlstm_prompt_kernel, n_chunks=tt // CHUNK),
        grid=(batch, nt),
        in_specs=[
            pl.BlockSpec((tt, W_A_QK), rb(COL_QA // W_A_QK)),
            pl.BlockSpec((tt, W_A_QK), rb(COL_KA // W_A_QK)),
            pl.BlockSpec((tt, W_A_V), rb(COL_VA // W_A_V)),
            pl.BlockSpec((tt, W_A_V), rb(COL_OA // W_A_V)),
            pl.BlockSpec((tt, LANES), rb(0)),
            pl.BlockSpec((SUBLANES, LANES), const),
            pl.BlockSpec((1, W_A_V), const),
        ],
        out_specs=[
            pl.BlockSpec((tt, W_A_V), rb(0)),
            pl.BlockSpec((1, H_A, DK_A, DV_A), lambda b, t: (b, 0, 0, 0)),
            pl.BlockSpec((1, H_A, DK_A), lambda b, t: (b, 0, 0)),
            pl.BlockSpec((1, SUBLANES, LANES), lambda b, t: (b, 0, 0)),
        ],
        out_shape=[
            jax.ShapeDtypeStruct((batch * seq, W_A_V), F32),
            jax.ShapeDtypeStruct((batch, H_A, DK_A, DV_A), F32),
            jax.ShapeDtypeStruct((batch, H_A, DK_A), F32),
            jax.ShapeDtypeStruct((batch, SUBLANES, LANES), F32),
        ],
        scratch_shapes=[
            pltpu.VMEM((H_A, DK_A, DV_A), F32),
            pltpu.VMEM((SUBLANES, LANES), F32),
            pltpu.VMEM((SUBLANES, LANES), F32),
        ],
        compiler_params=pltpu.CompilerParams(
            dimension_semantics=("parallel", "arbitrary"), vmem_limit_bytes=VMEM_LIMIT),
        name="mlstm_prompt",
    )(proj_main, proj_main, proj_main, proj_main, proj_small, prm, gna)


def _rep_rows(x, seg):
    g = x.shape[0]
    return jnp.concatenate([jnp.broadcast_to(x[j:j + 1, :], (seg, x.shape[1])) for j in range(g)], axis=0)


def _mlstm_sample_kernel(q_ref, k_ref, v_ref, o_ref, sm_ref, prm_ref, gna_ref, c0_ref, n0_ref, m0_ref,
                         ha_ref, c_out, n_out, m_out, *, seg):
    n = q_ref.shape[0]
    nb = n // seg
    lower, _ = _seg_masks(n, seg)
    lower_bf = _one_hot_bf(lower)
    eye_bf = _one_hot_bf(_iota((LANES, LANES), 0) == _iota((LANES, LANES), 1))
    scale = DK_A ** -0.5
    pm, cs, pm_t, cs_t = _gate_tables(sm_ref[...], prm_ref[...], "mlstm", lower_bf, eye_bf)
    m_rep = _rep_rows(m0_ref[...], seg)
    for h in range(H_A):
        qh = q_ref[:, h * DK_A:(h + 1) * DK_A] * scale
        kh = k_ref[:, h * DK_A:(h + 1) * DK_A]
        vh = v_ref[:, h * DV_A:(h + 1) * DV_A]
        i_row = pm[:, SM_I + h:SM_I + h + 1]
        b_row = cs[:, SM_F + h:SM_F + h + 1]
        i_lane = pm_t[SM_I + h:SM_I + h + 1, :]
        b_lane = cs_t[SM_F + h:SM_F + h + 1, :]
        m_rows = m_rep[:, h:h + 1]
        s, w_inter, m_row = _mlstm_intra(qh, kh, i_row, i_lane, b_row, b_lane, m_rows, lower)
        qc = jnp.concatenate(
            [_dot(qh[j * seg:(j + 1) * seg, :], c0_ref[j, h]) for j in range(nb)], axis=0)
        n_rep = _rep_rows(n0_ref[:, h, :], seg)
        num = w_inter * qc + _dot(s, vh)
        nq = w_inter * jnp.sum(qh * n_rep, axis=1, keepdims=True) + jnp.sum(s, axis=1, keepdims=True)
        ha_ref[:, h * DV_A:(h + 1) * DV_A] = _mlstm_out(
            num, nq, m_row, gna_ref[:, h * DV_A:(h + 1) * DV_A], o_ref[:, h * DV_A:(h + 1) * DV_A])
        for j in range(nb):
            sl = slice(j * seg, (j + 1) * seg)
            last = slice((j + 1) * seg - 1, (j + 1) * seg)
            m_new = m_row[last, :]
            b_last = b_row[last, :]
            m_prev = m0_ref[j:j + 1, h:h + 1]
            dec = jnp.exp(b_last + m_prev - m_new)
            kw = kh[sl, :] * jnp.exp(i_row[sl, :] + b_last - b_row[sl, :] - m_new)
            c_out[j, h] = dec * c0_ref[j, h] + _dot_tn(kw, vh[sl, :])
            n_out[j, h:h + 1, :] = dec * n0_ref[j, h:h + 1, :] + jnp.sum(kw, axis=0, keepdims=True)
            m_out[j:j + 1, h:h + 1] = m_new


def _mlstm_sample(proj_main, proj_small, prm, gna, c0, n0, m0, layer, batch, seq):
    nb = CHUNK // seq
    n = nb * seq
    rb = lambda col: (lambda g: (g, col))
    const = lambda g: (0, 0)
    return pl.pallas_call(
        functools.partial(_mlstm_sample_kernel, seg=seq),
        grid=(batch // nb,),
        in_specs=[
            pl.BlockSpec((n, W_A_QK), rb(COL_QA // W_A_QK)),
            pl.BlockSpec((n, W_A_QK), rb(COL_KA // W_A_QK)),
            pl.BlockSpec((n, W_A_V), rb(COL_VA // W_A_V)),
            pl.BlockSpec((n, W_A_V), rb(COL_OA // W_A_V)),
            pl.BlockSpec((n, LANES), rb(0)),
            pl.BlockSpec((SUBLANES, LANES), const),
            pl.BlockSpec((1, W_A_V), const),
            pl.BlockSpec((None, nb, H_A, DK_A, DV_A), lambda g: (layer, g, 0, 0, 0)),
            pl.BlockSpec((None, nb, H_A, DK_A), lambda g: (layer, g, 0, 0)),
            pl.BlockSpec((None, nb, H_A), lambda g: (layer, g, 0)),
        ],
        out_specs=[
            pl.BlockSpec((n, W_A_V), rb(0)),
            pl.BlockSpec((nb, H_A, DK_A, DV_A), lambda g: (g, 0, 0, 0)),
            pl.BlockSpec((nb, H_A, DK_A), lambda g: (g, 0, 0)),
            pl.BlockSpec((nb, H_A), lambda g: (g, 0)),
        ],
        out_shape=[
            jax.ShapeDtypeStruct((batch * seq, W_A_V), F32),
            jax.ShapeDtypeStruct((batch, H_A, DK_A, DV_A), F32),
            jax.ShapeDtypeStruct((batch, H_A, DK_A), F32),
            jax.ShapeDtypeStruct((batch, H_A), F32),
        ],
        compiler_params=pltpu.CompilerParams(
            dimension_semantics=("parallel",), vmem_limit_bytes=VMEM_LIMIT),
        name="mlstm_sample",
    )(proj_main, proj_main, proj_main, proj_main, proj_small, prm, gna, c0, n0, m0)


def _conv_silu(xp_ref, w_ref, cv_ref, src0, dst0, n_rows):
    cb = 512
    for c0 in range(0, C_CONV, cb):
        cols = slice(c0, c0 + cb)
        acc = w_ref[0:1, cols] * xp_ref[src0:src0 + n_rows, cols]
        for j in range(1, CONV_W):
            acc = acc + w_ref[j:j + 1, cols] * xp_ref[src0 + j:src0 + j + n_rows, cols]
        cv_ref[dst0:dst0 + n_rows, cols] = acc * _sigmoid(acc)


def _gdn_intra(cq, ck, vv, gam_row, gam_lane, beta_row, lower, strict, n_sq):
    n = cq.shape[0]
    qn = cq * (lax.rsqrt(jnp.sum(cq * cq, axis=1, keepdims=True) + EPS) * (DK_B ** -0.5))
    kn = ck * lax.rsqrt(jnp.sum(ck * ck, axis=1, keepdims=True) + EPS)
    decay = jnp.exp(jnp.where(lower, gam_row - gam_lane, -jnp.inf))
    kb = kn * beta_row
    nm = jnp.where(strict, -(_dot_nt(kb, kn) * decay), 0.0)
    eye = jnp.where(_iota((n, n), 0) == _iota((n, n), 1), 1.0, 0.0)
    tinv = eye + nm
    pw = nm
    for _ in range(n_sq):
        pw = _dot_x3(pw, pw)
        tinv = tinv + _dot_x3(tinv, pw)
    eg = jnp.exp(gam_row)
    sol = _dot_x3(tinv, jnp.concatenate([vv * beta_row, kb * eg], axis=1))
    w_v, w_k = sol[:, :DV_B], sol[:, DV_B:]
    qk = _dot_nt(qn, kn) * decay
    return qn * eg, kn, w_v, w_k, qk


def _gdn_out(o, gnb, z):
    return o * lax.rsqrt(jnp.mean(o * o, axis=1, keepdims=True) + EPS) * gnb * (z * _sigmoid(z))


def _gdn_prompt_kernel(x_ref, z_ref, sm_ref, cw_ref, prm_ref, gnb_ref,
                       hb_ref, s_out, conv_out, s_s, xp_s, cv_s, *, n_chunks):
    t = pl.program_id(1)
    L = CHUNK
    tt = n_chunks * L

    @pl.when(t == 0)
    def _():
        s_s[...] = jnp.zeros_like(s_s)
        xp_s[0:SUBLANES, :] = jnp.zeros((SUBLANES, C_CONV), F32)

    xp_s[SUBLANES:SUBLANES + tt, :] = x_ref[...]
    for c in range(n_chunks):
        _conv_silu(xp_s, cw_ref, cv_s, SUBLANES - (CONV_W - 1) + c * L, c * L, L)

    lower, strict = _seg_masks(L, L)
    lower_bf = _one_hot_bf(lower)
    eye_bf = _one_hot_bf(_iota((LANES, LANES), 0) == _iota((LANES, LANES), 1))
    prm = prm_ref[...]
    n_sq = int(math.log2(L)) - 1

    def chunk(c, carry):
        rows = pl.ds(pl.multiple_of(c * L, L), L)
        pm, cs, _, cs_t = _gate_tables(sm_ref[rows, :], prm, "gdn", lower_bf, eye_bf)
        for h in range(H_B):
            cq = cv_s[rows, h * DK_B:(h + 1) * DK_B]
            ck = cv_s[rows, W_B_QK + h * DK_B:W_B_QK + (h + 1) * DK_B]
            vv = cv_s[rows, 2 * W_B_QK + h * DV_B:2 * W_B_QK + (h + 1) * DV_B]
            gam_row = cs[:, SM_A + h:SM_A + h + 1]
            gam_lane = cs_t[SM_A + h:SM_A + h + 1, :]
            beta_row = pm[:, SM_BETA + h:SM_BETA + h + 1]
            q_dec, kn, w_v, w_k, qk = _gdn_intra(cq, ck, vv, gam_row, gam_lane, beta_row, lower, strict, n_sq)
            g_l = gam_row[L - 1:L, :]
            k_dec = kn * jnp.exp(g_l - gam_row)
            sh = s_s[h]
            u = w_v - _dot(w_k, sh)
            o = _dot(q_dec, sh) + _dot(qk, u)
            s_s[h] = sh * jnp.exp(g_l) + _dot_tn(k_dec, u)
            hb_ref[rows, h * DV_B:(h + 1) * DV_B] = _gdn_out(o, gnb_ref[...], z_ref[rows, h * DV_B:(h + 1) * DV_B])
        return carry

    lax.fori_loop(0, n_chunks, chunk, 0)

    xp_s[0:SUBLANES, :] = xp_s[tt:tt + SUBLANES, :]

    @pl.when(t == pl.num_programs(1) - 1)
    def _():
        s_out[0] = s_s[...]
        conv_out[0] = xp_s[SUBLANES - (CONV_W - 1):SUBLANES, :]


def _gdn_prompt(proj_main, proj_small, conv_w, prm, gnb, batch, seq):
    tt = 256
    nt = seq // tt
    rb = lambda col: (lambda b, t: (b * nt + t, col))
    const = lambda b, t: (0, 0)
    return pl.pallas_call(
        functools.partial(_gdn_prompt_kernel, n_chunks=tt // CHUNK),
        grid=(batch, nt),
        in_specs=[
            pl.BlockSpec((tt, C_CONV), rb(COL_QKVB // C_CONV)),
            pl.BlockSpec((tt, W_B_V), rb(COL_ZB // W_B_V)),
            pl.BlockSpec((tt, LANES), rb(0)),
            pl.BlockSpec((CONV_W, C_CONV), const),
            pl.BlockSpec((SUBLANES, LANES), const),
            pl.BlockSpec((1, DV_B), const),
        ],
        out_specs=[
            pl.BlockSpec((tt, W_B_V), rb(0)),
            pl.BlockSpec((1, H_B, DK_B, DV_B), lambda b, t: (b, 0, 0, 0)),
            pl.BlockSpec((1, CONV_W - 1, C_CONV), lambda b, t: (b, 0, 0)),
        ],
        out_shape=[
            jax.ShapeDtypeStruct((batch * seq, W_B_V), F32),
            jax.ShapeDtypeStruct((batch, H_B, DK_B, DV_B), F32),
            jax.ShapeDtypeStruct((batch, CONV_W - 1, C_CONV), F32),
        ],
        scratch_shapes=[
            pltpu.VMEM((H_B, DK_B, DV_B), F32),
            pltpu.VMEM((tt + SUBLANES, C_CONV), F32),
            pltpu.VMEM((tt, C_CONV), F32),
        ],
        compiler_params=pltpu.CompilerParams(
            dimension_semantics=("parallel", "arbitrary"), vmem_limit_bytes=VMEM_LIMIT),
        name="gdn_prompt",
    )(proj_main, proj_main, proj_small, conv_w, prm, gnb)


def _gdn_sample_kernel(x_ref, z_ref, sm_ref, cw_ref, prm_ref, gnb_ref, s0_ref, cb_ref,
                       hb_ref, s_out, conv_out, xp_s, cv_s, *, seg):
    n = x_ref.shape[0]
    nb = n // seg
    stride = 2 * SUBLANES
    for j in range(nb):
        base = j * stride + SUBLANES
        xp_s[base - (CONV_W - 1):base, :] = cb_ref[j]
        xp_s[base:base + seg, :] = x_ref[j * seg:(j + 1) * seg, :]
        _conv_silu(xp_s, cw_ref, cv_s, base - (CONV_W - 1), j * seg, seg)
        conv_out[j] = xp_s[base + seg - (CONV_W - 1):base + seg, :]

    lower, strict = _seg_masks(n, seg)
    lower_bf = _one_hot_bf(lower)
    eye_bf = _one_hot_bf(_iota((LANES, LANES), 0) == _iota((LANES, LANES), 1))
    n_sq = int(math.log2(seg)) - 1
    pm, cs, _, cs_t = _gate_tables(sm_ref[...], prm_ref[...], "gdn", lower_bf, eye_bf)
    for h in range(H_B):
        cq = cv_s[:, h * DK_B:(h + 1) * DK_B]
        ck = cv_s[:, W_B_QK + h * DK_B:W_B_QK + (h + 1) * DK_B]
        vv = cv_s[:, 2 * W_B_QK + h * DV_B:2 * W_B_QK + (h + 1) * DV_B]
        gam_row = cs[:, SM_A + h:SM_A + h + 1]
        gam_lane = cs_t[SM_A + h:SM_A + h + 1, :]
        beta_row = pm[:, SM_BETA + h:SM_BETA + h + 1]
        q_dec, kn, w_v, w_k, qk = _gdn_intra(cq, ck, vv, gam_row, gam_lane, beta_row, lower, strict, n_sq)
        us, qs = [], []
        for j in range(nb):
            sl = slice(j * seg, (j + 1) * seg)
            s0 = s0_ref[j, h]
            us.append(w_v[sl, :] - _dot(w_k[sl, :], s0))
            qs.append(_dot(q_dec[sl, :], s0))
        u = jnp.concatenate(us, axis=0)
        o = jnp.concatenate(qs, axis=0) + _dot(qk, u)
        hb_ref[:, h * DV_B:(h + 1) * DV_B] = _gdn_out(o, gnb_ref[...], z_ref[:, h * DV_B:(h + 1) * DV_B])
        for j in range(nb):
            sl = slice(j * seg, (j + 1) * seg)
            g_l = gam_row[(j + 1) * seg - 1:(j + 1) * seg, :]
            k_dec = kn[sl, :] * jnp.exp(g_l - gam_row[sl, :])
            s_out[j, h] = s0_ref[j, h] * jnp.exp(g_l) + _dot_tn(k_dec, us[j])


def _gdn_sample(proj_main, proj_small, conv_w, prm, gnb, s0, cb, layer, batch, seq):
    nb = CHUNK // seq
    n = nb * seq
    rb = lambda col: (lambda g: (g, col))
    const = lambda g: (0, 0)
    return pl.pallas_call(
        functools.partial(_gdn_sample_kernel, seg=seq),
        grid=(batch // nb,),
        in_specs=[
            pl.BlockSpec((n, C_CONV), rb(COL_QKVB // C_CONV)),
            pl.BlockSpec((n, W_B_V), rb(COL_ZB // W_B_V)),
            pl.BlockSpec((n, LANES), rb(0)),
            pl.BlockSpec((CONV_W, C_CONV), const),
            pl.BlockSpec((SUBLANES, LANES), const),
            pl.BlockSpec((1, DV_B), const),
            pl.BlockSpec((None, nb, H_B, DK_B, DV_B), lambda g: (layer, g, 0, 0, 0)),
            pl.BlockSpec((None, nb, CONV_W - 1, C_CONV), lambda g: (layer, g, 0, 0)),
        ],
        out_specs=[
            pl.BlockSpec((n, W_B_V), rb(0)),
            pl.BlockSpec((nb, H_B, DK_B, DV_B), lambda g: (g, 0, 0, 0)),
            pl.BlockSpec((nb, CONV_W - 1, C_CONV), lambda g: (g, 0, 0)),
        ],
        out_shape=[
            jax.ShapeDtypeStruct((batch * seq, W_B_V), F32),
            jax.ShapeDtypeStruct((batch, H_B, DK_B, DV_B), F32),
            jax.ShapeDtypeStruct((batch, CONV_W - 1, C_CONV), F32),
        ],
        scratch_shapes=[
            pltpu.VMEM((nb * 2 * SUBLANES, C_CONV), F32),
            pltpu.VMEM((n, C_CONV), F32),
        ],
        compiler_params=pltpu.CompilerParams(
            dimension_semantics=("parallel",), vmem_limit_bytes=VMEM_LIMIT),
        name="gdn_sample",
    )(proj_main, proj_main, proj_small, conv_w, prm, gnb, s0, cb)


def _pack_w_in(w_in):
    o = 0
    parts = {}
    for name, width in (("qa", W_A_QK), ("ka", W_A_QK), ("va", W_A_V), ("ip", H_A), ("fp", H_A), ("oa", W_A_V),
                        ("qkvb", C_CONV), ("ab", H_B), ("betab", H_B), ("zb", W_B_V), ("ga", D_MODEL),
                        ("gb", D_MODEL)):
        parts[name] = w_in[:, :, o:o + width]
        o += width
    main = jnp.concatenate([parts[k] for k in ("qa", "ka", "va", "oa", "qkvb", "zb", "ga", "gb")], axis=-1)
    small = jnp.concatenate([parts[k] for k in ("ip", "fp", "ab", "betab")], axis=-1)
    small = jnp.pad(small, ((0, 0), (0, 0), (0, LANES - small.shape[-1])))
    return main.astype(BF16), small.astype(BF16)


def _param_rows(b_igate, b_fgate, dt_bias, a_log):
    depth = b_igate.shape[0]
    bias = jnp.concatenate([b_igate, b_fgate, dt_bias], axis=-1)
    bias = jnp.pad(bias, ((0, 0), (0, LANES - bias.shape[-1])))
    alog = jnp.pad(a_log, ((0, 0), (SM_A, LANES - SM_A - H_B)))
    rows = jnp.stack([bias, alog], axis=1)
    return jnp.pad(rows, ((0, 0), (0, SUBLANES - 2), (0, 0))).astype(F32).reshape(depth, SUBLANES, LANES)


def kernel(x_prompt, x_sample, p_prompt, p_sample, state_mlstm_C, state_mlstm_n, state_mlstm_m, state_gdn_S,
           state_gdn_conv, g_mix, w_in, b_igate, b_fgate, g_norm_a, conv_w, a_log, dt_bias, g_norm_b, w_pa, w_pb,
           w_out, g_ffn, w_up, w_down, g_ple, w_pg, w_pp, g_final):
    depth = w_in.shape[0]
    bp, tp, _ = x_prompt.shape
    bs, ts, _ = x_sample.shape
    w_main, w_small = _pack_w_in(w_in)
    prm = _param_rows(b_igate, b_fgate, dt_bias, a_log)
    wb = {k: v.astype(BF16) for k, v in (("w_pa", w_pa), ("w_pb", w_pb), ("w_out", w_out), ("w_up", w_up),
                                           ("w_down", w_down), ("w_pg", w_pg), ("w_pp", w_pp))}
    xp = x_prompt.reshape(bp * tp, D_MODEL)
    xs = x_sample.reshape(bs * ts, D_MODEL)
    outs = {k: [] for k in ("pC", "pn", "pm", "pS", "pc", "sC", "sn", "sm", "sS", "sc")}
    for i in range(depth):
        g_in = g_mix[i].reshape(1, D_MODEL)
        gna = g_norm_a[i].reshape(1, W_A_V)
        gnb = g_norm_b[i].reshape(1, DV_B)
        lw = {k: v[i] for k, v in wb.items()}
        lw["g_ffn"] = g_ffn[i].reshape(1, D_MODEL)
        lw["g_ple"] = g_ple[i].reshape(1, D_MODEL)
        lw["g_final"] = g_final.reshape(1, D_MODEL)
        final = i == depth - 1

        pj, pj_s = _in_proj(xp, g_in, w_main[i], w_small[i])
        ha, c1, n1, m1 = _mlstm_prompt(pj, pj_s, prm[i], gna, bp, tp)
        hb, s1, v1 = _gdn_prompt(pj, pj_s, conv_w[i], prm[i], gnb, bp, tp)
        xp = _post(xp, ha, hb, pj, p_prompt[i].reshape(bp * tp, PLE_DIM), lw, final)
        outs["pC"].append(c1)
        outs["pn"].append(n1)
        outs["pm"].append(m1[:, :H_A, 0])
        outs["pS"].append(s1)
        outs["pc"].append(v1)

        qj, qj_s = _in_proj(xs, g_in, w_main[i], w_small[i])
        ha, c2, n2, m2 = _mlstm_sample(qj, qj_s, prm[i], gna, state_mlstm_C, state_mlstm_n, state_mlstm_m,
                                       i, bs, ts)
        hb, s2, v2 = _gdn_sample(qj, qj_s, conv_w[i], prm[i], gnb, state_gdn_S, state_gdn_conv, i, bs, ts)
        xs = _post(xs, ha, hb, qj, p_sample[i].reshape(bs * ts, PLE_DIM), lw, final)
        outs["sC"].append(c2)
        outs["sn"].append(n2)
        outs["sm"].append(m2)
        outs["sS"].append(s2)
        outs["sc"].append(v2)

    st = {k: jnp.stack(v) for k, v in outs.items()}
    return (xp.reshape(bp, tp, D_MODEL), xs.reshape(bs, ts, D_MODEL),
            st["pC"], st["pn"], st["pm"], st["pS"], st["pc"],
            st["sC"], st["sn"], st["sm"], st["sS"], st["sc"])
```

```python
import functools
import math

import jax
import jax.numpy as jnp
from jax import lax
from jax.experimental import pallas as pl
from jax.experimental.pallas import tpu as pltpu

F32 = jnp.float32
BF16 = jnp.bfloat16

D_MODEL = 1024
H_A, DK_A, DV_A = 4, 128, 256
H_B, DK_B, DV_B = 8, 128, 128
W_A_QK, W_A_V = H_A * DK_A, H_A * DV_A
W_B_QK, W_B_V = H_B * DK_B, H_B * DV_B
CONV_W = 4
C_CONV = 2 * W_B_QK + W_B_V
D_FF = 4 * D_MODEL
PLE_DIM = 256
CHUNK = 64
EPS = 1e-6

LANES = 128
SUBLANES = 8

COL_QA, COL_KA, COL_VA, COL_OA = 0, 512, 1024, 2048
COL_QKVB, COL_ZB, COL_GA, COL_GB = 3072, 6144, 7168, 8192
N_MAIN = 9216
SM_I, SM_F, SM_A, SM_BETA = 0, 4, 8, 16

VMEM_LIMIT = 56 * 1024 * 1024


def _bf(x):
    return x.astype(BF16)


def _mm(a, b):
    return jnp.dot(a, b, preferred_element_type=F32)


def _mm_nt(a, b):
    return lax.dot_general(a, b, (((1,), (1,)), ((), ())), preferred_element_type=F32)


def _mm_tn(a, b):
    return lax.dot_general(a, b, (((0,), (0,)), ((), ())), preferred_element_type=F32)


def _dot(a, b):
    return _mm(_bf(a), _bf(b))


def _dot_nt(a, b):
    return _mm_nt(_bf(a), _bf(b))


def _dot_tn(a, b):
    return _mm_tn(_bf(a), _bf(b))


def _split2(x):
    hi = _bf(x)
    lo = _bf(x - hi.astype(F32))
    return hi, lo


def _split3(x):
    hi = _bf(x)
    r = x - hi.astype(F32)
    mid = _bf(r)
    lo = _bf(r - mid.astype(F32))
    return hi, mid, lo


def _dot_x3(a, b):
    ah, al = _split2(a)
    bh, bl = _split2(b)
    return _mm(ah, bh) + _mm(ah, bl) + _mm(al, bh)


def _sigmoid(x):
    return 1.0 / (1.0 + jnp.exp(-x))


def _softplus(x):
    return jnp.maximum(x, 0.0) + jnp.log1p(jnp.exp(-jnp.abs(x)))


def _rmsnorm(x, g):
    return x * lax.rsqrt(jnp.mean(x * x, axis=-1, keepdims=True) + EPS) * g


def _iota(shape, dim):
    return lax.broadcasted_iota(jnp.int32, shape, dim)


def _seg_masks(n, seg):
    row, col = _iota((n, n), 0), _iota((n, n), 1)
    same = True if seg == n else (row // seg) == (col // seg)
    lower = jnp.logical_and(col <= row, same)
    strict = jnp.logical_and(col < row, same)
    upper = jnp.logical_and(col >= row, same)
    return lower, strict, upper, row == col


def _one_hot_bf(mask):
    return jnp.where(mask, 1.0, 0.0).astype(BF16)


def _in_kernel(x_ref, g_ref, wm_ref, ws_ref, om_ref, os_ref, h_ref):
    @pl.when(pl.program_id(1) == 0)
    def _():
        h_ref[...] = _bf(_rmsnorm(x_ref[...], g_ref[...]))
        os_ref[...] = _mm(h_ref[...], ws_ref[...])

    om_ref[...] = _mm(h_ref[...], wm_ref[...])


def _in_proj(x, g, w_main, w_small, layer):
    m = x.shape[0]
    tm = min(m, 1024)
    tn = 1024
    return pl.pallas_call(
        _in_kernel,
        grid=(m // tm, N_MAIN // tn),
        in_specs=[
            pl.BlockSpec((tm, D_MODEL), lambda i, j: (i, 0)),
            pl.BlockSpec((None, 1, D_MODEL), lambda i, j: (layer, 0, 0)),
            pl.BlockSpec((None, D_MODEL, tn), lambda i, j: (layer, 0, j)),
            pl.BlockSpec((None, D_MODEL, LANES), lambda i, j: (layer, 0, 0)),
        ],
        out_specs=[
            pl.BlockSpec((tm, tn), lambda i, j: (i, j)),
            pl.BlockSpec((tm, LANES), lambda i, j: (i, 0)),
        ],
        out_shape=[jax.ShapeDtypeStruct((m, N_MAIN), F32), jax.ShapeDtypeStruct((m, LANES), F32)],
        scratch_shapes=[pltpu.VMEM((tm, D_MODEL), BF16)],
        compiler_params=pltpu.CompilerParams(
            dimension_semantics=("parallel", "arbitrary"), vmem_limit_bytes=VMEM_LIMIT),
        name="in_proj",
    )(x, g, w_main, w_small)


def _post_kernel(x_ref, ha_ref, hb_ref, ga_ref, gb_ref, p_ref, wpa_ref, wpb_ref, wout_ref, gffn_ref,
                 wup_ref, wdown_ref, gple_ref, wpg_ref, wpp_ref, gfin_ref, o_ref, *, final):
    x = x_ref[...]
    a = _mm(_bf(ha_ref[...]), wpa_ref[...])
    b = _mm(_bf(hb_ref[...]), wpb_ref[...])
    mix = _sigmoid(ga_ref[...]) * a + _sigmoid(gb_ref[...]) * b
    x = x + _mm(_bf(mix), wout_ref[...])
    hn = _bf(_rmsnorm(x, gffn_ref[...]))
    acc = jnp.zeros_like(x)
    fc = 1024
    for c in range(D_FF // fc):
        u = _mm(hn, wup_ref[:, c * fc:(c + 1) * fc])
        r = jnp.maximum(u, 0.0)
        acc = acc + _mm(_bf(r * r), wdown_ref[c * fc:(c + 1) * fc, :])
    x = x + acc
    hp = _bf(_rmsnorm(x, gple_ref[...]))
    gate = _sigmoid(_mm(hp, wpg_ref[...]))
    x = x + gate * _mm(_bf(p_ref[...]), wpp_ref[...])
    if final:
        x = _rmsnorm(x, gfin_ref[...])
    o_ref[...] = x


def _post(x, ha, hb, proj_main, p, w, layer, final):
    m = x.shape[0]
    tm = min(m, 256)
    row = lambda i: (i, 0)
    wspec = lambda r, c: pl.BlockSpec((None, r, c), lambda i: (layer, 0, 0), pipeline_mode=pl.Buffered(1))
    return pl.pallas_call(
        functools.partial(_post_kernel, final=final),
        grid=(m // tm,),
        in_specs=[
            pl.BlockSpec((tm, D_MODEL), row),
            pl.BlockSpec((tm, W_A_V), row),
            pl.BlockSpec((tm, W_B_V), row),
            pl.BlockSpec((tm, D_MODEL), lambda i: (i, COL_GA // D_MODEL)),
            pl.BlockSpec((tm, D_MODEL), lambda i: (i, COL_GB // D_MODEL)),
            pl.BlockSpec((None, tm, PLE_DIM), lambda i: (layer, i, 0)),
            wspec(W_A_V, D_MODEL),
            wspec(W_B_V, D_MODEL),
            wspec(D_MODEL, D_MODEL),
            wspec(1, D_MODEL),
            wspec(D_MODEL, D_FF),
            wspec(D_FF, D_MODEL),
            wspec(1, D_MODEL),
            wspec(D_MODEL, D_MODEL),
            wspec(PLE_DIM, D_MODEL),
            pl.BlockSpec((1, D_MODEL), lambda i: (0, 0), pipeline_mode=pl.Buffered(1)),
        ],
        out_specs=pl.BlockSpec((tm, D_MODEL), row),
        out_shape=jax.ShapeDtypeStruct((m, D_MODEL), F32),
        compiler_params=pltpu.CompilerParams(
            dimension_semantics=("parallel",), vmem_limit_bytes=VMEM_LIMIT),
        name="post_final" if final else "post",
    )(x, ha, hb, proj_main, proj_main, p, w["w_pa"], w["w_pb"], w["w_out"], w["g_ffn"], w["w_up"],
      w["w_down"], w["g_ple"], w["w_pg"], w["w_pp"], w["g_final"])


def _gate_tables(small, prm, kind, lower_bf, upper_bf, eye_bf):
    n = small.shape[0]
    lane = _iota((n, LANES), 1)
    p = small + prm[0:1, :]
    if kind == "mlstm":
        lf = -_softplus(-p)
        pm = jnp.where(lane < SM_F, p, jnp.where(lane < SM_A, lf, 0.0))
    else:
        g = -jnp.exp(prm[1:2, :]) * _softplus(p)
        beta = _sigmoid(p)
        in_a = jnp.logical_and(lane >= SM_A, lane < SM_BETA)
        in_b = jnp.logical_and(lane >= SM_BETA, lane < SM_BETA + H_B)
        pm = jnp.where(in_a, g, jnp.where(in_b, beta, 0.0))
    parts = _split3(pm)
    cs = sum(_mm(lower_bf, q) for q in parts)
    pm_t = sum(_mm_tn(q, eye_bf) for q in parts)
    cs_t = sum(_mm_tn(q, upper_bf) for q in parts)
    return pm, cs, pm_t, cs_t


def _mlstm_gates(i_row, i_lane, b_row, b_lane, m_rows, lower):
    dlog = jnp.where(lower, b_row - b_lane + i_lane, -jnp.inf)
    inter = b_row + m_rows
    m_row = jnp.maximum(inter, jnp.max(dlog, axis=1, keepdims=True))
    return jnp.exp(dlog - m_row), jnp.exp(inter - m_row), m_row


def _mlstm_out(num, nq, m_row, gna, o):
    den = jnp.maximum(jnp.abs(nq), jnp.exp(-m_row))
    hh = num / den
    hr = hh * lax.rsqrt(jnp.mean(hh * hh, axis=1, keepdims=True) + EPS)
    return hr * gna * _sigmoid(o)


def _mlstm_prompt_kernel(q_ref, k_ref, v_ref, o_ref, sm_ref, prm_ref, gna_ref,
                         ha_ref, c_out, n_out, m_out, c_s, n_s, m_s, *, n_chunks):
    t = pl.program_id(1)
    L = CHUNK
    hs = range(H_A)

    @pl.when(t == 0)
    def _():
        c_s[...] = jnp.zeros_like(c_s)
        n_s[...] = jnp.zeros_like(n_s)
        m_s[...] = jnp.zeros_like(m_s)

    lower, _, upper, diag = _seg_masks(L, L)
    lower_bf, upper_bf, eye_bf = _one_hot_bf(lower), _one_hot_bf(upper), _one_hot_bf(diag)
    prm = prm_ref[...]
    scale = DK_A ** -0.5

    def chunk(c, carry):
        rows = pl.ds(pl.multiple_of(c * L, L), L)
        pm, cs, pm_t, cs_t = _gate_tables(sm_ref[rows, :], prm, "mlstm", lower_bf, upper_bf, eye_bf)
        qh = [q_ref[rows, h * DK_A:(h + 1) * DK_A] * scale for h in hs]
        kh = [k_ref[rows, h * DK_A:(h + 1) * DK_A] for h in hs]
        vh = [_bf(v_ref[rows, h * DV_A:(h + 1) * DV_A]) for h in hs]
        i_row = [pm[:, SM_I + h:SM_I + h + 1] for h in hs]
        b_row = [cs[:, SM_F + h:SM_F + h + 1] for h in hs]
        m_prev = [m_s[h:h + 1, 0:1] for h in hs]
        gates = [_mlstm_gates(i_row[h], pm_t[SM_I + h:SM_I + h + 1, :], b_row[h],
                              cs_t[SM_F + h:SM_F + h + 1, :], m_prev[h], lower) for h in hs]
        m_new = [gates[h][2][L - 1:L, :] for h in hs]
        b_last = [b_row[h][L - 1:L, :] for h in hs]
        dec = [jnp.exp(b_last[h] + m_prev[h] - m_new[h]) for h in hs]
        kw = [kh[h] * jnp.exp(i_row[h] + b_last[h] - b_row[h] - m_new[h]) for h in hs]
        ch = [c_s[h] for h in hs]
        nh = [n_s[h:h + 1, :] for h in hs]
        qk = [_dot_nt(qh[h], kh[h]) for h in hs]
        qc = [_dot(qh[h], ch[h]) for h in hs]
        kv = [_mm_tn(_bf(kw[h]), vh[h]) for h in hs]
        s = [qk[h] * gates[h][0] for h in hs]
        sv = [_mm(_bf(s[h]), vh[h]) for h in hs]
        for h in hs:
            w_inter, m_row = gates[h][1], gates[h][2]
            num = w_inter * qc[h] + sv[h]
            nq = (w_inter * jnp.sum(qh[h] * nh[h], axis=1, keepdims=True)
                  + jnp.sum(s[h], axis=1, keepdims=True))
            ha_ref[rows, h * DV_A:(h + 1) * DV_A] = _mlstm_out(
                num, nq, m_row, gna_ref[:, h * DV_A:(h + 1) * DV_A], o_ref[rows, h * DV_A:(h + 1) * DV_A])
            c_s[h] = dec[h] * ch[h] + kv[h]
            n_s[h:h + 1, :] = dec[h] * nh[h] + jnp.sum(kw[h], axis=0, keepdims=True)
            m_s[h:h + 1, :] = jnp.broadcast_to(m_new[h], (1, LANES))
        return carry

    lax.fori_loop(0, n_chunks, chunk, 0)

    @pl.when(t == pl.num_programs(1) - 1)
    def _():
        c_out[0] = c_s[...]
        n_out[0] = n_s[0:H_A, :]
        m_out[0] = m_s[...]


def _mlstm_prompt(proj_main, proj_small, prm, gna, layer, batch, seq):
    tt = 256
    nt = seq // tt
    rb = lambda col: (lambda b, t: (b * nt + t, col))
    par = lambda b, t: (layer, 0, 0)
    return pl.pallas_call(
        functools.partial(_mlstm_prompt_kernel, n_chunks=tt // CHUNK),
        grid=(batch, nt),
        in_specs=[
            pl.BlockSpec((tt, W_A_QK), rb(COL_QA // W_A_QK)),
            pl.BlockSpec((tt, W_A_QK), rb(COL_KA // W_A_QK)),
            pl.BlockSpec((tt, W_A_V), rb(COL_VA // W_A_V)),
            pl.BlockSpec((tt, W_A_V), rb(COL_OA // W_A_V)),
            pl.BlockSpec((tt, LANES), rb(0)),
            pl.BlockSpec((None, SUBLANES, LANES), par),
            pl.BlockSpec((None, 1, W_A_V), par),
        ],
        out_specs=[
            pl.BlockSpec((tt, W_A_V), rb(0)),
            pl.BlockSpec((1, H_A, DK_A, DV_A), lambda b, t: (b, 0, 0, 0)),
            pl.BlockSpec((1, H_A, DK_A), lambda b, t: (b, 0, 0)),
            pl.BlockSpec((1, SUBLANES, LANES), lambda b, t: (b, 0, 0)),
        ],
        out_shape=[
            jax.ShapeDtypeStruct((batch * seq, W_A_V), F32),
            jax.ShapeDtypeStruct((batch, H_A, DK_A, DV_A), F32),
            jax.ShapeDtypeStruct((batch, H_A, DK_A), F32),
            jax.ShapeDtypeStruct((batch, SUBLANES, LANES), F32),
        ],
        scratch_shapes=[
            pltpu.VMEM((H_A, DK_A, DV_A), F32),
            pltpu.VMEM((SUBLANES, LANES), F32),
            pltpu.VMEM((SUBLANES, LANES), F32),
        ],
        compiler_params=pltpu.CompilerParams(
            dimension_semantics=("parallel", "arbitrary"), vmem_limit_bytes=VMEM_LIMIT),
        name="mlstm_prompt",
    )(proj_main, proj_main, proj_main, proj_main, proj_small, prm, gna)


def _rep_rows(x, seg):
    g = x.shape[0]
    return jnp.concatenate([jnp.broadcast_to(x[j:j + 1, :], (seg, x.shape[1])) for j in range(g)], axis=0)


def _mlstm_sample_kernel(q_ref, k_ref, v_ref, o_ref, sm_ref, prm_ref, gna_ref, c0_ref, n0_ref, m0_ref,
                         ha_ref, c_out, n_out, m_out, *, seg):
    n = q_ref.shape[0]
    nb = n // seg
    hs, js = range(H_A), range(nb)
    sl = [slice(j * seg, (j + 1) * seg) for j in js]
    last = [slice((j + 1) * seg - 1, (j + 1) * seg) for j in js]
    lower, _, upper, diag = _seg_masks(n, seg)
    lower_bf, upper_bf, eye_bf = _one_hot_bf(lower), _one_hot_bf(upper), _one_hot_bf(diag)
    scale = DK_A ** -0.5
    pm, cs, pm_t, cs_t = _gate_tables(sm_ref[...], prm_ref[...], "mlstm", lower_bf, upper_bf, eye_bf)
    m_rep = _rep_rows(m0_ref[...], seg)
    qh = [q_ref[:, h * DK_A:(h + 1) * DK_A] * scale for h in hs]
    kh = [k_ref[:, h * DK_A:(h + 1) * DK_A] for h in hs]
    vh = [_bf(v_ref[:, h * DV_A:(h + 1) * DV_A]) for h in hs]
    i_row = [pm[:, SM_I + h:SM_I + h + 1] for h in hs]
    b_row = [cs[:, SM_F + h:SM_F + h + 1] for h in hs]
    gates = [_mlstm_gates(i_row[h], pm_t[SM_I + h:SM_I + h + 1, :], b_row[h],
                          cs_t[SM_F + h:SM_F + h + 1, :], m_rep[:, h:h + 1], lower) for h in hs]
    qk = [_dot_nt(qh[h], kh[h]) for h in hs]
    qc = [jnp.concatenate([_dot(qh[h][sl[j], :], c0_ref[j, h]) for j in js], axis=0) for h in hs]
    s = [qk[h] * gates[h][0] for h in hs]
    sv = [_mm(_bf(s[h]), vh[h]) for h in hs]
    for h in hs:
        w_inter, m_row = gates[h][1], gates[h][2]
        n_rep = _rep_rows(n0_ref[:, h, :], seg)
        num = w_inter * qc[h] + sv[h]
        nq = w_inter * jnp.sum(qh[h] * n_rep, axis=1, keepdims=True) + jnp.sum(s[h], axis=1, keepdims=True)
        ha_ref[:, h * DV_A:(h + 1) * DV_A] = _mlstm_out(
            num, nq, m_row, gna_ref[:, h * DV_A:(h + 1) * DV_A], o_ref[:, h * DV_A:(h + 1) * DV_A])
    for h in hs:
        m_row = gates[h][2]
        for j in js:
            m_new = m_row[last[j], :]
            b_last = b_row[h][last[j], :]
            dec = jnp.exp(b_last + m0_ref[j:j + 1, h:h + 1] - m_new)
            kw = kh[h][sl[j], :] * jnp.exp(i_row[h][sl[j], :] + b_last - b_row[h][sl[j], :] - m_new)
            c_out[j, h] = dec * c0_ref[j, h] + _mm_tn(_bf(kw), vh[h][sl[j], :])
            n_out[j, h:h + 1, :] = dec * n0_ref[j, h:h + 1, :] + jnp.sum(kw, axis=0, keepdims=True)
            m_out[j:j + 1, h:h + 1] = m_new


def _mlstm_sample(proj_main, proj_small, prm, gna, c0, n0, m0, layer, batch, seq):
    nb = CHUNK // seq
    n = nb * seq
    rb = lambda col: (lambda g: (g, col))
    par = lambda g: (layer, 0, 0)
    return pl.pallas_call(
        functools.partial(_mlstm_sample_kernel, seg=seq),
        grid=(batch // nb,),
        in_specs=[
            pl.BlockSpec((n, W_A_QK), rb(COL_QA // W_A_QK)),
            pl.BlockSpec((n, W_A_QK), rb(COL_KA // W_A_QK)),
            pl.BlockSpec((n, W_A_V), rb(COL_VA // W_A_V)),
            pl.BlockSpec((n, W_A_V), rb(COL_OA // W_A_V)),
            pl.BlockSpec((n, LANES), rb(0)),
            pl.BlockSpec((None, SUBLANES, LANES), par),
            pl.BlockSpec((None, 1, W_A_V), par),
            pl.BlockSpec((None, nb, H_A, DK_A, DV_A), lambda g: (layer, g, 0, 0, 0)),
            pl.BlockSpec((None, nb, H_A, DK_A), lambda g: (layer, g, 0, 0)),
            pl.BlockSpec((None, nb, H_A), lambda g: (layer, g, 0)),
        ],
        out_specs=[
            pl.BlockSpec((n, W_A_V), rb(0)),
            pl.BlockSpec((nb, H_A, DK_A, DV_A), lambda g: (g, 0, 0, 0)),
            pl.BlockSpec((nb, H_A, DK_A), lambda g: (g, 0, 0)),
            pl.BlockSpec((nb, H_A), lambda g: (g, 0)),
        ],
        out_shape=[
            jax.ShapeDtypeStruct((batch * seq, W_A_V), F32),
            jax.ShapeDtypeStruct((batch, H_A, DK_A, DV_A), F32),
            jax.ShapeDtypeStruct((batch, H_A, DK_A), F32),
            jax.ShapeDtypeStruct((batch, H_A), F32),
        ],
        compiler_params=pltpu.CompilerParams(
            dimension_semantics=("parallel",), vmem_limit_bytes=VMEM_LIMIT),
        name="mlstm_sample",
    )(proj_main, proj_main, proj_main, proj_main, proj_small, prm, gna, c0, n0, m0)


def _conv_silu(xp_ref, w_ref, cv_ref, src0, dst0, n_rows):
    cb = 512
    for c0 in range(0, C_CONV, cb):
        cols = slice(c0, c0 + cb)
        acc = w_ref[0:1, cols] * xp_ref[src0:src0 + n_rows, cols]
        for j in range(1, CONV_W):
            acc = acc + w_ref[j:j + 1, cols] * xp_ref[src0 + j:src0 + j + n_rows, cols]
        cv_ref[dst0:dst0 + n_rows, cols] = acc * _sigmoid(acc)


def _gdn_intra(cv, pm, cs, cs_t, lower, strict, diag, n_sq):
    hs = range(H_B)
    gam_row = [cs[:, SM_A + h:SM_A + h + 1] for h in hs]
    beta_row = [pm[:, SM_BETA + h:SM_BETA + h + 1] for h in hs]
    decay = [jnp.exp(jnp.where(lower, gam_row[h] - cs_t[SM_A + h:SM_A + h + 1, :], -jnp.inf)) for h in hs]
    eg = [jnp.exp(gam_row[h]) for h in hs]
    qn, kn, kb, rhs = [], [], [], []
    for h in hs:
        cq, ck, vv = cv(h, 0), cv(h, 1), cv(h, 2)
        qn.append(cq * (lax.rsqrt(jnp.sum(cq * cq, axis=1, keepdims=True) + EPS) * (DK_B ** -0.5)))
        kn.append(ck * lax.rsqrt(jnp.sum(ck * ck, axis=1, keepdims=True) + EPS))
        kb.append(kn[h] * beta_row[h])
        rhs.append(jnp.concatenate([vv * beta_row[h], kb[h] * eg[h]], axis=1))
    kn_bf = [_bf(kn[h]) for h in hs]
    kk = [_mm_nt(_bf(kb[h]), kn_bf[h]) for h in hs]
    qk = [_mm_nt(_bf(qn[h]), kn_bf[h]) for h in hs]
    eye = jnp.where(diag, 1.0, 0.0)
    pw = [jnp.where(strict, -(kk[h] * decay[h]), 0.0) for h in hs]
    tinv = [eye + pw[h] for h in hs]
    for _ in range(n_sq):
        pw = [_dot_x3(pw[h], pw[h]) for h in hs]
        tinv = [tinv[h] + _dot_x3(tinv[h], pw[h]) for h in hs]
    sol = [_dot_x3(tinv[h], rhs[h]) for h in hs]
    w_v = [sol[h][:, :DV_B] for h in hs]
    w_k = [_bf(sol[h][:, DV_B:]) for h in hs]
    qk = [_bf(qk[h] * decay[h]) for h in hs]
    q_dec = [_bf(qn[h] * eg[h]) for h in hs]
    return q_dec, kn, w_v, w_k, qk, gam_row


def _gdn_out(o, gnb, z):
    return o * lax.rsqrt(jnp.mean(o * o, axis=1, keepdims=True) + EPS) * gnb * (z * _sigmoid(z))


def _gdn_prompt_kernel(x_ref, z_ref, sm_ref, cw_ref, prm_ref, gnb_ref,
                       hb_ref, s_out, conv_out, s_s, xp_s, cv_s, *, n_chunks):
    t = pl.program_id(1)
    L = CHUNK
    tt = n_chunks * L
    hs = range(H_B)

    @pl.when(t == 0)
    def _():
        s_s[...] = jnp.zeros_like(s_s)
        xp_s[0:SUBLANES, :] = jnp.zeros((SUBLANES, C_CONV), F32)

    xp_s[SUBLANES:SUBLANES + tt, :] = x_ref[...]
    for c in range(n_chunks):
        _conv_silu(xp_s, cw_ref, cv_s, SUBLANES - (CONV_W - 1) + c * L, c * L, L)

    lower, strict, upper, diag = _seg_masks(L, L)
    lower_bf, upper_bf, eye_bf = _one_hot_bf(lower), _one_hot_bf(upper), _one_hot_bf(diag)
    prm = prm_ref[...]
    n_sq = int(math.log2(L)) - 1

    def chunk(c, carry):
        rows = pl.ds(pl.multiple_of(c * L, L), L)
        pm, cs, _, cs_t = _gate_tables(sm_ref[rows, :], prm, "gdn", lower_bf, upper_bf, eye_bf)
        cv = lambda h, part: cv_s[rows, part * W_B_QK + h * DK_B:part * W_B_QK + (h + 1) * DK_B]
        q_dec, kn, w_v, w_k, qk, gam_row = _gdn_intra(cv, pm, cs, cs_t, lower, strict, diag, n_sq)
        g_l = [gam_row[h][L - 1:L, :] for h in hs]
        k_dec = [_bf(kn[h] * jnp.exp(g_l[h] - gam_row[h])) for h in hs]
        sh = [s_s[h] for h in hs]
        sh_bf = [_bf(sh[h]) for h in hs]
        u = [w_v[h] - _mm(w_k[h], sh_bf[h]) for h in hs]
        oq = [_mm(q_dec[h], sh_bf[h]) for h in hs]
        u_bf = [_bf(u[h]) for h in hs]
        o = [oq[h] + _mm(qk[h], u_bf[h]) for h in hs]
        for h in hs:
            s_s[h] = sh[h] * jnp.exp(g_l[h]) + _mm_tn(k_dec[h], u_bf[h])
        for h in hs:
            hb_ref[rows, h * DV_B:(h + 1) * DV_B] = _gdn_out(
                o[h], gnb_ref[...], z_ref[rows, h * DV_B:(h + 1) * DV_B])
        return carry

    lax.fori_loop(0, n_chunks, chunk, 0)

    xp_s[0:SUBLANES, :] = xp_s[tt:tt + SUBLANES, :]

    @pl.when(t == pl.num_programs(1) - 1)
    def _():
        s_out[0] = s_s[...]
        conv_out[0] = xp_s[SUBLANES - (CONV_W - 1):SUBLANES, :]


def _gdn_prompt(proj_main, proj_small, conv_w, prm, gnb, layer, batch, seq):
    tt = 256
    nt = seq // tt
    rb = lambda col: (lambda b, t: (b * nt + t, col))
    par = lambda b, t: (layer, 0, 0)
    return pl.pallas_call(
        functools.partial(_gdn_prompt_kernel, n_chunks=tt // CHUNK),
        grid=(batch, nt),
        in_specs=[
            pl.BlockSpec((tt, C_CONV), rb(COL_QKVB // C_CONV)),
            pl.BlockSpec((tt, W_B_V), rb(COL_ZB // W_B_V)),
            pl.BlockSpec((tt, LANES), rb(0)),
            pl.BlockSpec((None, CONV_W, C_CONV), par),
            pl.BlockSpec((None, SUBLANES, LANES), par),
            pl.BlockSpec((None, 1, DV_B), par),
        ],
        out_specs=[
            pl.BlockSpec((tt, W_B_V), rb(0)),
            pl.BlockSpec((1, H_B, DK_B, DV_B), lambda b, t: (b, 0, 0, 0)),
            pl.BlockSpec((1, CONV_W - 1, C_CONV), lambda b, t: (b, 0, 0)),
        ],
        out_shape=[
            jax.ShapeDtypeStruct((batch * seq, W_B_V), F32),
            jax.ShapeDtypeStruct((batch, H_B, DK_B, DV_B), F32),
            jax.ShapeDtypeStruct((batch, CONV_W - 1, C_CONV), F32),
        ],
        scratch_shapes=[
            pltpu.VMEM((H_B, DK_B, DV_B), F32),
            pltpu.VMEM((tt + SUBLANES, C_CONV), F32),
            pltpu.VMEM((tt, C_CONV), F32),
        ],
        compiler_params=pltpu.CompilerParams(
            dimension_semantics=("parallel", "arbitrary"), vmem_limit_bytes=VMEM_LIMIT),
        name="gdn_prompt",
    )(proj_main, proj_main, proj_small, conv_w, prm, gnb)


def _gdn_sample_kernel(x_ref, z_ref, sm_ref, cw_ref, prm_ref, gnb_ref, s0_ref, cb_ref,
                       hb_ref, s_out, conv_out, xp_s, cv_s, *, seg):
    n = x_ref.shape[0]
    nb = n // seg
    hs, js = range(H_B), range(nb)
    sl = [slice(j * seg, (j + 1) * seg) for j in js]
    stride = 2 * SUBLANES
    for j in js:
        base = j * stride + SUBLANES
        xp_s[base - (CONV_W - 1):base, :] = cb_ref[j]
        xp_s[base:base + seg, :] = x_ref[sl[j], :]
        _conv_silu(xp_s, cw_ref, cv_s, base - (CONV_W - 1), j * seg, seg)
        conv_out[j] = xp_s[base + seg - (CONV_W - 1):base + seg, :]

    lower, strict, upper, diag = _seg_masks(n, seg)
    lower_bf, upper_bf, eye_bf = _one_hot_bf(lower), _one_hot_bf(upper), _one_hot_bf(diag)
    n_sq = int(math.log2(seg)) - 1
    pm, cs, _, cs_t = _gate_tables(sm_ref[...], prm_ref[...], "gdn", lower_bf, upper_bf, eye_bf)
    cv = lambda h, part: cv_s[:, part * W_B_QK + h * DK_B:part * W_B_QK + (h + 1) * DK_B]
    q_dec, kn, w_v, w_k, qk, gam_row = _gdn_intra(cv, pm, cs, cs_t, lower, strict, diag, n_sq)
    s0_bf = [[_bf(s0_ref[j, h]) for j in js] for h in hs]
    us = [[w_v[h][sl[j], :] - _mm(w_k[h][sl[j], :], s0_bf[h][j]) for j in js] for h in hs]
    oq = [jnp.concatenate([_mm(q_dec[h][sl[j], :], s0_bf[h][j]) for j in js], axis=0) for h in hs]
    u_bf = [_bf(jnp.concatenate(us[h], axis=0)) for h in hs]
    o = [oq[h] + _mm(qk[h], u_bf[h]) for h in hs]
    for h in hs:
        hb_ref[:, h * DV_B:(h + 1) * DV_B] = _gdn_out(o[h], gnb_ref[...], z_ref[:, h * DV_B:(h + 1) * DV_B])
    for h in hs:
        for j in js:
            g_l = gam_row[h][(j + 1) * seg - 1:(j + 1) * seg, :]
            k_dec = kn[h][sl[j], :] * jnp.exp(g_l - gam_row[h][sl[j], :])
            s_out[j, h] = s0_ref[j, h] * jnp.exp(g_l) + _mm_tn(_bf(k_dec), u_bf[h][sl[j], :])


def _gdn_sample(proj_main, proj_small, conv_w, prm, gnb, s0, cb, layer, batch, seq):
    nb = CHUNK // seq
    n = nb * seq
    rb = lambda col: (lambda g: (g, col))
    par = lambda g: (layer, 0, 0)
    return pl.pallas_call(
        functools.partial(_gdn_sample_kernel, seg=seq),
        grid=(batch // nb,),
        in_specs=[
            pl.BlockSpec((n, C_CONV), rb(COL_QKVB // C_CONV)),
            pl.BlockSpec((n, W_B_V), rb(COL_ZB // W_B_V)),
            pl.BlockSpec((n, LANES), rb(0)),
            pl.BlockSpec((None, CONV_W, C_CONV), par),
            pl.BlockSpec((None, SUBLANES, LANES), par),
            pl.BlockSpec((None, 1, DV_B), par),
            pl.BlockSpec((None, nb, H_B, DK_B, DV_B), lambda g: (layer, g, 0, 0, 0)),
            pl.BlockSpec((None, nb, CONV_W - 1, C_CONV), lambda g: (layer, g, 0, 0)),
        ],
        out_specs=[
            pl.BlockSpec((n, W_B_V), rb(0)),
            pl.BlockSpec((nb, H_B, DK_B, DV_B), lambda g: (g, 0, 0, 0)),
            pl.BlockSpec((nb, CONV_W - 1, C_CONV), lambda g: (g, 0, 0)),
        ],
        out_shape=[
            jax.ShapeDtypeStruct((batch * seq, W_B_V), F32),
            jax.ShapeDtypeStruct((batch, H_B, DK_B, DV_B), F32),
            jax.ShapeDtypeStruct((batch, CONV_W - 1, C_CONV), F32),
        ],
        scratch_shapes=[
            pltpu.VMEM((nb * 2 * SUBLANES, C_CONV), F32),
            pltpu.VMEM((n, C_CONV), F32),
        ],
        compiler_params=pltpu.CompilerParams(
            dimension_semantics=("parallel",), vmem_limit_bytes=VMEM_LIMIT),
        name="gdn_sample",
    )(proj_main, proj_main, proj_small, conv_w, prm, gnb, s0, cb)


def _pack_w_in(w_in):
    w = w_in.astype(BF16)
    o = 0
    parts = {}
    for name, width in (("qa", W_A_QK), ("ka", W_A_QK), ("va", W_A_V), ("ip", H_A), ("fp", H_A), ("oa", W_A_V),
                        ("qkvb", C_CONV), ("ab", H_B), ("betab", H_B), ("zb", W_B_V), ("ga", D_MODEL),
                        ("gb", D_MODEL)):
        parts[name] = w[:, :, o:o + width]
        o += width
    main = jnp.concatenate([parts[k] for k in ("qa", "ka", "va", "oa", "qkvb", "zb", "ga", "gb")], axis=-1)
    small = jnp.concatenate([parts[k] for k in ("ip", "fp", "ab", "betab")], axis=-1)
    small = jnp.pad(small, ((0, 0), (0, 0), (0, LANES - small.shape[-1])))
    return main, small


def _param_rows(b_igate, b_fgate, dt_bias, a_log):
    bias = jnp.concatenate([b_igate, b_fgate, dt_bias], axis=-1)
    bias = jnp.pad(bias, ((0, 0), (0, LANES - bias.shape[-1])))
    alog = jnp.pad(a_log, ((0, 0), (SM_A, LANES - SM_A - H_B)))
    rows = jnp.stack([bias, alog], axis=1)
    return jnp.pad(rows, ((0, 0), (0, SUBLANES - 2), (0, 0))).astype(F32)


def kernel(x_prompt, x_sample, p_prompt, p_sample, state_mlstm_C, state_mlstm_n, state_mlstm_m, state_gdn_S,
           state_gdn_conv, g_mix, w_in, b_igate, b_fgate, g_norm_a, conv_w, a_log, dt_bias, g_norm_b, w_pa, w_pb,
           w_out, g_ffn, w_up, w_down, g_ple, w_pg, w_pp, g_final):
    depth = w_in.shape[0]
    bp, tp, _ = x_prompt.shape
    bs, ts, _ = x_sample.shape
    w_main, w_small = _pack_w_in(w_in)
    prm = _param_rows(b_igate, b_fgate, dt_bias, a_log)
    w = {k: v.astype(BF16) for k, v in (("w_pa", w_pa), ("w_pb", w_pb), ("w_out", w_out), ("w_up", w_up),
                                          ("w_down", w_down), ("w_pg", w_pg), ("w_pp", w_pp))}
    w["g_ffn"] = g_ffn.reshape(depth, 1, D_MODEL)
    w["g_ple"] = g_ple.reshape(depth, 1, D_MODEL)
    w["g_final"] = g_final.reshape(1, D_MODEL)
    g_in = g_mix.reshape(depth, 1, D_MODEL)
    gna = g_norm_a.reshape(depth, 1, W_A_V)
    gnb = g_norm_b.reshape(depth, 1, DV_B)
    pp = p_prompt.reshape(depth, bp * tp, PLE_DIM)
    ps = p_sample.reshape(depth, bs * ts, PLE_DIM)
    xp = x_prompt.reshape(bp * tp, D_MODEL)
    xs = x_sample.reshape(bs * ts, D_MODEL)
    outs = {k: [] for k in ("pC", "pn", "pm", "pS", "pc", "sC", "sn", "sm", "sS", "sc")}
    for i in range(depth):
        final = i == depth - 1

        pj, pj_s = _in_proj(xp, g_in, w_main, w_small, i)
        ha, c1, n1, m1 = _mlstm_prompt(pj, pj_s, prm, gna, i, bp, tp)
        hb, s1, v1 = _gdn_prompt(pj, pj_s, conv_w, prm, gnb, i, bp, tp)
        xp = _post(xp, ha, hb, pj, pp, w, i, final)
        outs["pC"].append(c1)
        outs["pn"].append(n1)
        outs["pm"].append(m1[:, :H_A, 0])
        outs["pS"].append(s1)
        outs["pc"].append(v1)

        qj, qj_s = _in_proj(xs, g_in, w_main, w_small, i)
        ha, c2, n2, m2 = _mlstm_sample(qj, qj_s, prm, gna, state_mlstm_C, state_mlstm_n, state_mlstm_m,
                                       i, bs, ts)
        hb, s2, v2 = _gdn_sample(qj, qj_s, conv_w, prm, gnb, state_gdn_S, state_gdn_conv, i, bs, ts)
        xs = _post(xs, ha, hb, qj, ps, w, i, final)
        outs["sC"].append(c2)
        outs["sn"].append(n2)
        outs["sm"].append(m2)
        outs["sS"].append(s2)
        outs["sc"].append(v2)

    st = {k: jnp.stack(v) for k, v in outs.items()}
    return (xp.reshape(bp, tp, D_MODEL), xs.reshape(bs, ts, D_MODEL),
            st["pC"], st["pn"], st["pm"], st["pS"], st["pc"],
            st["sC"], st["sn"], st["sm"], st["sS"], st["sc"])
```

```python
import functools
import math

import jax
import jax.numpy as jnp
from jax import lax
from jax.experimental import pallas as pl
from jax.experimental.pallas import tpu as pltpu

F32 = jnp.float32
BF16 = jnp.bfloat16

D_MODEL = 1024
H_A, DK_A, DV_A = 4, 128, 256
H_B, DK_B, DV_B = 8, 128, 128
W_A_QK, W_A_V = H_A * DK_A, H_A * DV_A
W_B_QK, W_B_V = H_B * DK_B, H_B * DV_B
CONV_W = 4
C_CONV = 2 * W_B_QK + W_B_V
D_FF = 4 * D_MODEL
PLE_DIM = 256
CHUNK = 64
EPS = 1e-6

LANES = 128
SUBLANES = 8

COL_QA, COL_KA, COL_VA, COL_OA = 0, 512, 1024, 2048
COL_QKVB, COL_ZB, COL_GA, COL_GB = 3072, 6144, 7168, 8192
N_MAIN = 9216
SM_I, SM_F, SM_A, SM_BETA = 0, 4, 8, 16

TILE_T = 256
MLSTM_CHUNKS_PER_BODY = 1
VMEM_LIMIT = 56 * 1024 * 1024


def _bf(x):
    return x.astype(BF16)


def _f32(x):
    return x.astype(F32)


def _mm(a, b):
    return jnp.dot(a, b, preferred_element_type=F32)


def _mm_nt(a, b):
    return lax.dot_general(a, b, (((1,), (1,)), ((), ())), preferred_element_type=F32)


def _mm_tn(a, b):
    return lax.dot_general(a, b, (((0,), (0,)), ((), ())), preferred_element_type=F32)


def _split3(x):
    hi = _bf(x)
    r = x - _f32(hi)
    mid = _bf(r)
    lo = _bf(r - _f32(mid))
    return hi, mid, lo


def _sigmoid(x):
    return 1.0 / (1.0 + jnp.exp(-x))


def _softplus(x):
    return jnp.maximum(x, 0.0) + jnp.log1p(jnp.exp(-jnp.abs(x)))


def _rmsnorm(x, g):
    return x * lax.rsqrt(jnp.mean(x * x, axis=-1, keepdims=True) + EPS) * g


def _iota(shape, dim):
    return lax.broadcasted_iota(jnp.int32, shape, dim)


def _seg_masks(n, seg, reps=1):
    row = _iota((n, reps * n), 0)
    col = _iota((n, reps * n), 1) % n
    same = True if seg == n else (row // seg) == (col // seg)
    lower = jnp.logical_and(col <= row, same)
    strict = jnp.logical_and(col < row, same)
    upper = jnp.logical_and(col >= row, same)
    return lower, strict, upper, row == col


def _one_hot_bf(mask):
    return jnp.where(mask, 1.0, 0.0).astype(BF16)


def _run(*gens):
    active = list(gens)
    while active:
        for g in list(active):
            try:
                next(g)
            except StopIteration:
                active.remove(g)


def _in_kernel(x_ref, g_ref, wm_ref, ws_ref, om_ref, os_ref, h_ref):
    @pl.when(pl.program_id(1) == 0)
    def _():
        h_ref[...] = _bf(_rmsnorm(x_ref[...], g_ref[...]))
        os_ref[...] = _mm(h_ref[...], ws_ref[...])

    om_ref[...] = _bf(_mm(h_ref[...], wm_ref[...]))


def _in_proj(x, g, w_main, w_small, layer):
    m = x.shape[0]
    tm = min(m, 2048)
    tn = 1024
    return pl.pallas_call(
        _in_kernel,
        grid=(m // tm, N_MAIN // tn),
        in_specs=[
            pl.BlockSpec((tm, D_MODEL), lambda i, j: (i, 0)),
            pl.BlockSpec((None, 1, D_MODEL), lambda i, j: (layer, 0, 0)),
            pl.BlockSpec((None, D_MODEL, tn), lambda i, j: (layer, 0, j)),
            pl.BlockSpec((None, D_MODEL, LANES), lambda i, j: (layer, 0, 0)),
        ],
        out_specs=[
            pl.BlockSpec((tm, tn), lambda i, j: (i, j)),
            pl.BlockSpec((tm, LANES), lambda i, j: (i, 0)),
        ],
        out_shape=[jax.ShapeDtypeStruct((m, N_MAIN), BF16), jax.ShapeDtypeStruct((m, LANES), F32)],
        scratch_shapes=[pltpu.VMEM((tm, D_MODEL), BF16)],
        compiler_params=pltpu.CompilerParams(
            dimension_semantics=("parallel", "arbitrary"), vmem_limit_bytes=VMEM_LIMIT),
        name="in_proj",
    )(x, g, w_main, w_small)


def _post_kernel(x_ref, ha_ref, hb_ref, ga_ref, gb_ref, p_ref, wpa_ref, wpb_ref, wout_ref, gffn_ref,
                 wup_ref, wdown_ref, gple_ref, wpg_ref, wpp_ref, gfin_ref, o_ref, *, final):
    x = x_ref[...]
    a = _mm(ha_ref[...], wpa_ref[...])
    b = _mm(hb_ref[...], wpb_ref[...])
    mix = _sigmoid(_f32(ga_ref[...])) * a + _sigmoid(_f32(gb_ref[...])) * b
    x = x + _mm(_bf(mix), wout_ref[...])
    hn = _bf(_rmsnorm(x, gffn_ref[...]))
    acc = jnp.zeros_like(x)
    fc = 1024
    for c in range(D_FF // fc):
        u = _mm(hn, wup_ref[:, c * fc:(c + 1) * fc])
        r = jnp.maximum(u, 0.0)
        acc = acc + _mm(_bf(r * r), wdown_ref[c * fc:(c + 1) * fc, :])
    x = x + acc
    hp = _bf(_rmsnorm(x, gple_ref[...]))
    gate = _sigmoid(_mm(hp, wpg_ref[...]))
    x = x + gate * _mm(_bf(p_ref[...]), wpp_ref[...])
    if final:
        x = _rmsnorm(x, gfin_ref[...])
    o_ref[...] = x


def _post(x, ha, hb, proj_main, p, w, layer, final):
    m = x.shape[0]
    tm = min(m, 256)
    row = lambda i: (i, 0)
    wspec = lambda r, c: pl.BlockSpec((None, r, c), lambda i: (layer, 0, 0), pipeline_mode=pl.Buffered(1))
    return pl.pallas_call(
        functools.partial(_post_kernel, final=final),
        grid=(m // tm,),
        in_specs=[
            pl.BlockSpec((tm, D_MODEL), row),
            pl.BlockSpec((tm, W_A_V), row),
            pl.BlockSpec((tm, W_B_V), row),
            pl.BlockSpec((tm, D_MODEL), lambda i: (i, COL_GA // D_MODEL)),
            pl.BlockSpec((tm, D_MODEL), lambda i: (i, COL_GB // D_MODEL)),
            pl.BlockSpec((None, tm, PLE_DIM), lambda i: (layer, i, 0)),
            wspec(W_A_V, D_MODEL),
            wspec(W_B_V, D_MODEL),
            wspec(D_MODEL, D_MODEL),
            wspec(1, D_MODEL),
            wspec(D_MODEL, D_FF),
            wspec(D_FF, D_MODEL),
            wspec(1, D_MODEL),
            wspec(D_MODEL, D_MODEL),
            wspec(PLE_DIM, D_MODEL),
            pl.BlockSpec((1, D_MODEL), lambda i: (0, 0), pipeline_mode=pl.Buffered(1)),
        ],
        out_specs=pl.BlockSpec((tm, D_MODEL), row),
        out_shape=jax.ShapeDtypeStruct((m, D_MODEL), F32),
        compiler_params=pltpu.CompilerParams(
            dimension_semantics=("parallel",), vmem_limit_bytes=VMEM_LIMIT),
        name="post_final" if final else "post",
    )(x, ha, hb, proj_main, proj_main, p, w["w_pa"], w["w_pb"], w["w_out"], w["g_ffn"], w["w_up"],
      w["w_down"], w["g_ple"], w["w_pg"], w["w_pp"], w["g_final"])


def _gate_tables(small, prm, kind, lower_bf, upper_bf, eye_bf):
    n = small.shape[0]
    lane = _iota((n, LANES), 1)
    p = small + prm[0:1, :]
    if kind == "mlstm":
        lf = -_softplus(-p)
        pm = jnp.where(lane < SM_F, p, jnp.where(lane < SM_A, lf, 0.0))
    else:
        g = -jnp.exp(prm[1:2, :]) * _softplus(p)
        beta = _sigmoid(p)
        in_a = jnp.logical_and(lane >= SM_A, lane < SM_BETA)
        in_b = jnp.logical_and(lane >= SM_BETA, lane < SM_BETA + H_B)
        pm = jnp.where(in_a, g, jnp.where(in_b, beta, 0.0))
    parts = _split3(pm)
    cs = sum(_mm(lower_bf, q) for q in parts)
    pm_t = sum(_mm_tn(q, eye_bf) for q in parts) if eye_bf is not None else None
    cs_t = sum(_mm_tn(q, upper_bf) for q in parts)
    return pm, cs, pm_t, cs_t


def _mlstm_out(num, nq, m_row, gna, o):
    den = jnp.maximum(jnp.abs(nq), jnp.exp(-m_row))
    hh = num / den
    hr = hh * lax.rsqrt(jnp.mean(hh * hh, axis=1, keepdims=True) + EPS)
    return hr * gna * _sigmoid(o)


def _mlstm_prompt_kernel(q_ref, k_ref, v_ref, o_ref, sm_ref, prm_ref, gna_ref,
                         ha_ref, c_out, n_out, m_out, c_s, n_s, m_s, *, n_chunks):
    t = pl.program_id(1)
    L = CHUNK
    cpb = MLSTM_CHUNKS_PER_BODY
    hs, cs_ = range(H_A), range(cpb)
    items = [(c, h) for c in cs_ for h in hs]

    @pl.when(t == 0)
    def _():
        c_s[...] = jnp.zeros_like(c_s)
        n_s[...] = jnp.zeros_like(n_s)
        m_s[...] = jnp.zeros_like(m_s)

    lower, _, upper, diag = _seg_masks(L, L)
    lower_bf, upper_bf, eye_bf = _one_hot_bf(lower), _one_hot_bf(upper), _one_hot_bf(diag)
    prm = prm_ref[...]
    scale = DK_A ** -0.5

    def group(gi, carry):
        rows = [pl.ds(pl.multiple_of((gi * cpb + c) * L, L), L) for c in cs_]
        tabs = [_gate_tables(sm_ref[rows[c], :], prm, "mlstm", lower_bf, upper_bf, eye_bf) for c in cs_]
        qb = {(c, h): q_ref[rows[c], h * DK_A:(h + 1) * DK_A] for c, h in items}
        kb = {(c, h): k_ref[rows[c], h * DK_A:(h + 1) * DK_A] for c, h in items}
        vb = {(c, h): v_ref[rows[c], h * DV_A:(h + 1) * DV_A] for c, h in items}
        qk = {it: _mm_nt(qb[it], kb[it]) for it in items}
        i_row, b_row, a_row, s_loc, kw_loc = {}, {}, {}, {}, {}
        for c, h in items:
            pm, cs, pm_t, cs_t = tabs[c]
            i_row[c, h] = pm[:, SM_I + h:SM_I + h + 1]
            b_row[c, h] = cs[:, SM_F + h:SM_F + h + 1]
            dlog = jnp.where(
                lower, b_row[c, h] - cs_t[SM_F + h:SM_F + h + 1, :] + pm_t[SM_I + h:SM_I + h + 1, :], -jnp.inf)
            a_row[c, h] = jnp.max(dlog, axis=1, keepdims=True)
            s_loc[c, h] = qk[c, h] * (scale * jnp.exp(dlog - a_row[c, h]))
            b_last = b_row[c, h][L - 1:L, :]
            kw_loc[c, h] = _f32(kb[c, h]) * jnp.exp(
                i_row[c, h] + b_last - b_row[c, h] - a_row[c, h][L - 1:L, :])
        sv = {it: _mm(_bf(s_loc[it]), vb[it]) for it in items}
        kv = {it: _mm_tn(_bf(kw_loc[it]), vb[it]) for it in items}

        ch = [c_s[h] for h in hs]
        nh = [n_s[h:h + 1, :] for h in hs]
        mh = [m_s[h:h + 1, 0:1] for h in hs]
        rs = {it: jnp.sum(s_loc[it], axis=1, keepdims=True) for it in items}
        ks = {it: jnp.sum(kw_loc[it], axis=0, keepdims=True) for it in items}
        qc, qn, m_row, mprev = {}, {}, {}, {}
        for c in cs_:
            for h in hs:
                it = (c, h)
                mprev[it] = mh[h]
                qc[it] = _mm(qb[it], _bf(ch[h]))
                qn[it] = jnp.sum(_f32(qb[it]) * nh[h], axis=1, keepdims=True)
                m_row[it] = jnp.maximum(b_row[it] + mh[h], a_row[it])
                m_new = m_row[it][L - 1:L, :]
                dec = jnp.exp(b_row[it][L - 1:L, :] + mh[h] - m_new)
                g_loc = jnp.exp(a_row[it][L - 1:L, :] - m_new)
                ch[h] = dec * ch[h] + g_loc * kv[it]
                nh[h] = dec * nh[h] + g_loc * ks[it]
                mh[h] = m_new
        for c, h in items:
            it = (c, h)
            w_inter = scale * jnp.exp(b_row[it] + mprev[it] - m_row[it])
            f_intra = jnp.exp(a_row[it] - m_row[it])
            num = w_inter * qc[it] + f_intra * sv[it]
            nq = w_inter * qn[it] + f_intra * rs[it]
            ha_ref[rows[c], h * DV_A:(h + 1) * DV_A] = _bf(_mlstm_out(
                num, nq, m_row[it], gna_ref[:, h * DV_A:(h + 1) * DV_A],
                _f32(o_ref[rows[c], h * DV_A:(h + 1) * DV_A])))
        for h in hs:
            c_s[h] = ch[h]
            n_s[h:h + 1, :] = nh[h]
            m_s[h:h + 1, :] = jnp.broadcast_to(mh[h], (1, LANES))
        return carry

    lax.fori_loop(0, n_chunks // cpb, group, 0)

    @pl.when(t == pl.num_programs(1) - 1)
    def _():
        c_out[0] = c_s[...]
        n_out[0] = n_s[0:H_A, :]
        m_out[0] = m_s[...]


def _mlstm_prompt(proj_main, proj_small, prm, gna, layer, batch, seq):
    tt = TILE_T
    nt = seq // tt
    rb = lambda col: (lambda b, t: (b * nt + t, col))
    par = lambda b, t: (layer, 0, 0)
    return pl.pallas_call(
        functools.partial(_mlstm_prompt_kernel, n_chunks=tt // CHUNK),
        grid=(batch, nt),
        in_specs=[
            pl.BlockSpec((tt, W_A_QK), rb(COL_QA // W_A_QK)),
            pl.BlockSpec((tt, W_A_QK), rb(COL_KA // W_A_QK)),
            pl.BlockSpec((tt, W_A_V), rb(COL_VA // W_A_V)),
            pl.BlockSpec((tt, W_A_V), rb(COL_OA // W_A_V)),
            pl.BlockSpec((tt, LANES), rb(0)),
            pl.BlockSpec((None, SUBLANES, LANES), par),
            pl.BlockSpec((None, 1, W_A_V), par),
        ],
        out_specs=[
            pl.BlockSpec((tt, W_A_V), rb(0)),
            pl.BlockSpec((1, H_A, DK_A, DV_A), lambda b, t: (b, 0, 0, 0)),
            pl.BlockSpec((1, H_A, DK_A), lambda b, t: (b, 0, 0)),
            pl.BlockSpec((1, SUBLANES, LANES), lambda b, t: (b, 0, 0)),
        ],
        out_shape=[
            jax.ShapeDtypeStruct((batch * seq, W_A_V), BF16),
            jax.ShapeDtypeStruct((batch, H_A, DK_A, DV_A), F32),
            jax.ShapeDtypeStruct((batch, H_A, DK_A), F32),
            jax.ShapeDtypeStruct((batch, SUBLANES, LANES), F32),
        ],
        scratch_shapes=[
            pltpu.VMEM((H_A, DK_A, DV_A), F32),
            pltpu.VMEM((SUBLANES, LANES), F32),
            pltpu.VMEM((SUBLANES, LANES), F32),
        ],
        compiler_params=pltpu.CompilerParams(
            dimension_semantics=("parallel", "arbitrary"), vmem_limit_bytes=VMEM_LIMIT),
        name="mlstm_prompt",
    )(proj_main, proj_main, proj_main, proj_main, proj_small, prm, gna)


def _rep_rows(x, seg):
    g = x.shape[0]
    return jnp.concatenate([jnp.broadcast_to(x[j:j + 1, :], (seg, x.shape[1])) for j in range(g)], axis=0)


def _mlstm_sample_kernel(*refs, seg, aliased):
    (q_ref, k_ref, v_ref, o_ref, sm_ref, prm_ref, gna_ref, c0_ref, n0_ref, m0_ref) = refs[:10]
    ha_ref, c_out, n_out, m_out = refs[10 + aliased:]
    n = q_ref.shape[0]
    nb = n // seg
    hs, js = range(H_A), range(nb)
    sl = [slice(j * seg, (j + 1) * seg) for j in js]
    last = [slice((j + 1) * seg - 1, (j + 1) * seg) for j in js]
    lower, _, upper, diag = _seg_masks(n, seg)
    lower_bf, upper_bf, eye_bf = _one_hot_bf(lower), _one_hot_bf(upper), _one_hot_bf(diag)
    scale = DK_A ** -0.5
    pm, cs, pm_t, cs_t = _gate_tables(sm_ref[...], prm_ref[...], "mlstm", lower_bf, upper_bf, eye_bf)
    m_rep = _rep_rows(m0_ref[...], seg)
    qb = [q_ref[:, h * DK_A:(h + 1) * DK_A] for h in hs]
    kb = [k_ref[:, h * DK_A:(h + 1) * DK_A] for h in hs]
    vb = [v_ref[:, h * DV_A:(h + 1) * DV_A] for h in hs]
    i_row = [pm[:, SM_I + h:SM_I + h + 1] for h in hs]
    b_row = [cs[:, SM_F + h:SM_F + h + 1] for h in hs]
    qk = [_mm_nt(qb[h], kb[h]) for h in hs]
    qc = [jnp.concatenate([_mm(qb[h][sl[j], :], _bf(c0_ref[j, h])) for j in js], axis=0) for h in hs]
    m_row, w_inter, s = [], [], []
    for h in hs:
        dlog = jnp.where(lower, b_row[h] - cs_t[SM_F + h:SM_F + h + 1, :] + pm_t[SM_I + h:SM_I + h + 1, :],
                         -jnp.inf)
        inter = b_row[h] + m_rep[:, h:h + 1]
        m_row.append(jnp.maximum(inter, jnp.max(dlog, axis=1, keepdims=True)))
        w_inter.append(scale * jnp.exp(inter - m_row[h]))
        s.append(qk[h] * (scale * jnp.exp(dlog - m_row[h])))
    sv = [_mm(_bf(s[h]), vb[h]) for h in hs]
    for h in hs:
        n_rep = _rep_rows(n0_ref[:, h, :], seg)
        num = w_inter[h] * qc[h] + sv[h]
        nq = (w_inter[h] * jnp.sum(_f32(qb[h]) * n_rep, axis=1, keepdims=True)
              + jnp.sum(s[h], axis=1, keepdims=True))
        ha_ref[:, h * DV_A:(h + 1) * DV_A] = _bf(_mlstm_out(
            num, nq, m_row[h], gna_ref[:, h * DV_A:(h + 1) * DV_A], _f32(o_ref[:, h * DV_A:(h + 1) * DV_A])))
    for h in hs:
        kf = _f32(kb[h])
        for j in js:
            m_new = m_row[h][last[j], :]
            b_last = b_row[h][last[j], :]
            dec = jnp.exp(b_last + m0_ref[j:j + 1, h:h + 1] - m_new)
            kw = kf[sl[j], :] * jnp.exp(i_row[h][sl[j], :] + b_last - b_row[h][sl[j], :] - m_new)
            c_out[j, h] = dec * c0_ref[j, h] + _mm_tn(_bf(kw), vb[h][sl[j], :])
            n_out[j, h:h + 1, :] = dec * n0_ref[j, h:h + 1, :] + jnp.sum(kw, axis=0, keepdims=True)
            m_out[j:j + 1, h:h + 1] = m_new


def _mlstm_sample(proj_main, proj_small, prm, gna, c0, n0, m0, c_stack, layer, batch, seq):
    depth = c0.shape[0]
    nb = CHUNK // seq
    n = nb * seq
    rb = lambda col: (lambda g: (g, col))
    par = lambda g: (layer, 0, 0)
    aliased = c_stack is not None
    in_specs = [
        pl.BlockSpec((n, W_A_QK), rb(COL_QA // W_A_QK)),
        pl.BlockSpec((n, W_A_QK), rb(COL_KA // W_A_QK)),
        pl.BlockSpec((n, W_A_V), rb(COL_VA // W_A_V)),
        pl.BlockSpec((n, W_A_V), rb(COL_OA // W_A_V)),
        pl.BlockSpec((n, LANES), rb(0)),
        pl.BlockSpec((None, SUBLANES, LANES), par),
        pl.BlockSpec((None, 1, W_A_V), par),
        pl.BlockSpec((None, nb, H_A, DK_A, DV_A), lambda g: (layer, g, 0, 0, 0)),
        pl.BlockSpec((None, nb, H_A, DK_A), lambda g: (layer, g, 0, 0)),
        pl.BlockSpec((None, nb, H_A), lambda g: (layer, g, 0)),
    ]
    args = [proj_main, proj_main, proj_main, proj_main, proj_small, prm, gna, c0, n0, m0]
    if aliased:
        in_specs.append(pl.BlockSpec(memory_space=pl.ANY))
        args.append(c_stack)
    return pl.pallas_call(
        functools.partial(_mlstm_sample_kernel, seg=seq, aliased=int(aliased)),
        grid=(batch // nb,),
        in_specs=in_specs,
        out_specs=[
            pl.BlockSpec((n, W_A_V), rb(0)),
            pl.BlockSpec((None, nb, H_A, DK_A, DV_A), lambda g: (layer, g, 0, 0, 0)),
            pl.BlockSpec((nb, H_A, DK_A), lambda g: (g, 0, 0)),
            pl.BlockSpec((nb, H_A), lambda g: (g, 0)),
        ],
        out_shape=[
            jax.ShapeDtypeStruct((batch * seq, W_A_V), BF16),
            jax.ShapeDtypeStruct((depth, batch, H_A, DK_A, DV_A), F32),
            jax.ShapeDtypeStruct((batch, H_A, DK_A), F32),
            jax.ShapeDtypeStruct((batch, H_A), F32),
        ],
        input_output_aliases={10: 1} if aliased else {},
        compiler_params=pltpu.CompilerParams(
            dimension_semantics=("parallel",), vmem_limit_bytes=VMEM_LIMIT),
        name="mlstm_sample",
    )(*args)


def _conv_silu(xp_ref, w_ref, cv_ref, src0, dst0, n_rows):
    cb = 512
    top = src0 + CONV_W - 1 - SUBLANES
    for c0 in range(0, C_CONV, cb):
        cols = slice(c0, c0 + cb)
        xx = xp_ref[top:top + SUBLANES + n_rows, cols]
        acc = w_ref[CONV_W - 1:CONV_W, cols] * xx[SUBLANES:, :]
        for j in range(1, CONV_W):
            acc = acc + w_ref[CONV_W - 1 - j:CONV_W - j, cols] * pltpu.roll(xx, j, axis=0)[SUBLANES:, :]
        cv_ref[dst0:dst0 + n_rows, cols] = acc * _sigmoid(acc)


def _bd_rhs(x_f, first):
    return jnp.concatenate([_bf(jnp.where(first, x_f, 0.0)), _bf(jnp.where(first, 0.0, x_f))], axis=0)


def _pair_split(x2, first, want_lhs=True, want_rhs=True):
    hi = _bf(x2)
    hi_f = _f32(hi)
    lo_f = x2 - hi_f
    lhs3 = jnp.concatenate([hi, hi, _bf(lo_f)], axis=1) if want_lhs else None
    rhs3 = None
    if want_rhs:
        bd_hi = _bd_rhs(hi_f, first)
        rhs3 = jnp.concatenate([bd_hi, _bd_rhs(lo_f, first), bd_hi], axis=0)
    return lhs3, rhs3


def _bd2(a, b):
    z = jnp.zeros_like(a)
    return jnp.concatenate([jnp.concatenate([a, z], axis=1), jnp.concatenate([z, b], axis=1)], axis=0)


def _gdn_intra_gen(load_cv, tabs, masks, n_sq, res):
    lower2, strict2, diag2, first = masks
    items = [(b, p) for b in range(len(tabs)) for p in range(H_B // 2)]
    gam_row, decay2, eg, qn, kn, kbv, rhs, kq = {}, {}, {}, {}, {}, {}, {}, {}
    for b, p in items:
        pm, cs, _, cs_t2 = tabs[b]
        hh = (2 * p, 2 * p + 1)
        gam_row[b, p] = [cs[:, SM_A + h:SM_A + h + 1] for h in hh]
        beta_row = [pm[:, SM_BETA + h:SM_BETA + h + 1] for h in hh]
        gam_lane2 = jnp.where(first[0:1, :], cs_t2[SM_A + hh[0]:SM_A + hh[0] + 1, :],
                              cs_t2[SM_A + hh[1]:SM_A + hh[1] + 1, :])
        gam_row2 = jnp.where(first, gam_row[b, p][0], gam_row[b, p][1])
        decay2[b, p] = jnp.exp(jnp.where(lower2, gam_row2 - gam_lane2, -jnp.inf))
        eg[b, p] = [jnp.exp(g) for g in gam_row[b, p]]
        cq2, ck2, vv2 = load_cv(b, p, 0), load_cv(b, p, 1), load_cv(b, p, 2)
        qn[b, p], kn[b, p], kbv[b, p], rhs[b, p] = [], [], [], []
        for e in range(2):
            cq, ck, vv = (x[:, e * DK_B:(e + 1) * DK_B] for x in (cq2, ck2, vv2))
            qn[b, p].append(cq * (lax.rsqrt(jnp.sum(cq * cq, axis=1, keepdims=True) + EPS) * (DK_B ** -0.5)))
            kn[b, p].append(ck * lax.rsqrt(jnp.sum(ck * ck, axis=1, keepdims=True) + EPS))
            kbv[b, p].append(kn[b, p][e] * beta_row[e])
            rhs[b, p].append(jnp.concatenate([vv * beta_row[e], kbv[b, p][e] * eg[b, p][e]], axis=1))
        lhs = jnp.concatenate([_bf(jnp.concatenate(kbv[b, p], axis=1)), _bf(jnp.concatenate(qn[b, p], axis=1))],
                              axis=0)
        kq[b, p] = _mm_nt(lhs, _bd2(_bf(kn[b, p][0]), _bf(kn[b, p][1])))
    yield
    n = lower2.shape[0]
    eye2 = jnp.where(diag2, 1.0, 0.0)
    pw, tinv, l_pw, r_pw = {}, {}, {}, {}
    for it in items:
        pw[it] = jnp.where(strict2, -(kq[it][0:n, :] * decay2[it]), 0.0)
        tinv[it] = eye2 + pw[it]
        l_pw[it], r_pw[it] = _pair_split(pw[it], first)
    for step in range(n_sq):
        for it in items:
            pw[it] = _mm(l_pw[it], r_pw[it])
        yield
        for it in items:
            l_pw[it], r_pw[it] = _pair_split(pw[it], first, want_lhs=step + 1 < n_sq)
            tinv[it] = tinv[it] + _mm(_pair_split(tinv[it], first, want_rhs=False)[0], r_pw[it])
        yield
    sol = {}
    for it in items:
        parts = []
        for e in range(2):
            hi = _bf(rhs[it][e])
            parts.append((hi, _bf(rhs[it][e] - _f32(hi))))
        bd_hi = _bd2(parts[0][0], parts[1][0])
        rhs3 = jnp.concatenate([bd_hi, _bd2(parts[0][1], parts[1][1]), bd_hi], axis=0)
        sol[it] = _mm(_pair_split(tinv[it], first, want_rhs=False)[0], rhs3)
    yield
    for it in items:
        w_v = [sol[it][:, e * 2 * DV_B:e * 2 * DV_B + DV_B] for e in range(2)]
        w_k = [_bf(sol[it][:, e * 2 * DV_B + DV_B:(e + 1) * 2 * DV_B]) for e in range(2)]
        q_dec = [_bf(qn[it][e] * eg[it][e]) for e in range(2)]
        res[it] = (q_dec, kn[it], w_v, w_k, _bf(kq[it][n:, :] * decay2[it]), gam_row[it])


def _gdn_scan_pair(res_it, s_pair, g_l):
    q_dec, kn, w_v, w_k, qk2, gam_row = res_it
    n = w_v[0].shape[0]
    r = [_mm(jnp.concatenate([w_k[e], q_dec[e]], axis=0), _bf(s_pair[e])) for e in range(2)]
    yield
    u = [_bf(w_v[e] - r[e][0:n, :]) for e in range(2)]
    o2 = jnp.concatenate([r[0][n:, :], r[1][n:, :]], axis=1) + _mm(qk2, _bd2(u[0], u[1]))
    for e in range(2):
        k_dec = _bf(kn[e] * jnp.exp(g_l[e] - gam_row[e]))
        s_pair[e] = s_pair[e] * jnp.exp(g_l[e]) + _mm_tn(k_dec, u[e])
    yield
    return o2


def _gdn_out(o, gnb, z):
    return o * lax.rsqrt(jnp.mean(o * o, axis=1, keepdims=True) + EPS) * gnb * (z * _sigmoid(z))


def _gdn_prompt_kernel(x_ref, z_ref, sm_ref, cw_ref, prm_ref, gnb_ref,
                       hb_ref, s_out, conv_out, s_s, xp_s, cv_s, *, n_chunks):
    t = pl.program_id(1)
    L = CHUNK
    tt = n_chunks * L
    ps = range(H_B // 2)
    rows = [slice(c * L, (c + 1) * L) for c in range(n_chunks)]

    @pl.when(t == 0)
    def _():
        s_s[...] = jnp.zeros_like(s_s)
        xp_s[0:SUBLANES, :] = jnp.zeros((SUBLANES, C_CONV), F32)

    xp_s[SUBLANES:SUBLANES + tt, :] = _f32(x_ref[...])
    for c in range(n_chunks):
        _conv_silu(xp_s, cw_ref, cv_s, SUBLANES - (CONV_W - 1) + c * L, c * L, L)

    lower, _, _, _ = _seg_masks(L, L)
    lower2, strict2, upper2, diag2 = _seg_masks(L, L, reps=2)
    first = _iota((L, 2 * L), 1) < L
    masks = (lower2, strict2, diag2, first)
    lower_bf, upper2_bf = _one_hot_bf(lower), _one_hot_bf(upper2)
    prm = prm_ref[...]
    n_sq = int(math.log2(L)) - 1
    state = [[s_s[2 * p], s_s[2 * p + 1]] for p in ps]
    res, outs = {}, {}

    def intra(chunks):
        tabs = [_gate_tables(sm_ref[rows[c], :], prm, "gdn", lower_bf, upper2_bf, None) for c in chunks]
        load = lambda b, p, part: cv_s[rows[chunks[b]],
                                       part * W_B_QK + 2 * p * DK_B:part * W_B_QK + 2 * (p + 1) * DK_B]
        sub = {}
        yield from _gdn_intra_gen(load, tabs, masks, n_sq, sub)
        for (b, p), v in sub.items():
            res[chunks[b], p] = v

    def scan(chunks):
        for c in chunks:
            gens = []
            for p in ps:
                g_l = [g[L - 1:L, :] for g in res[c, p][5]]
                gens.append(_gdn_scan_pair(res[c, p], state[p], g_l))
            for stage in range(2):
                for g in gens:
                    next(g)
                yield
            for p, g in zip(ps, gens):
                try:
                    next(g)
                except StopIteration as done:
                    outs[c, p] = done.value

    def emit(chunks):
        for c in chunks:
            for p in ps:
                for e in range(2):
                    h = 2 * p + e
                    hb_ref[rows[c], h * DV_B:(h + 1) * DV_B] = _bf(_gdn_out(
                        outs[c, p][:, e * DV_B:(e + 1) * DV_B], gnb_ref[...],
                        _f32(z_ref[rows[c], h * DV_B:(h + 1) * DV_B])))

    half = n_chunks // 2
    first_half, second_half = list(range(half)), list(range(half, n_chunks))
    _run(intra(first_half))
    _run(intra(second_half), scan(first_half))
    emit(first_half)
    _run(scan(second_half))
    emit(second_half)
    for p in ps:
        s_s[2 * p] = state[p][0]
        s_s[2 * p + 1] = state[p][1]

    xp_s[0:SUBLANES, :] = xp_s[tt:tt + SUBLANES, :]

    @pl.when(t == pl.num_programs(1) - 1)
    def _():
        s_out[0] = s_s[...]
        conv_out[0] = xp_s[SUBLANES - (CONV_W - 1):SUBLANES, :]


def _gdn_prompt(proj_main, proj_small, conv_w, prm, gnb, layer, batch, seq):
    tt = TILE_T
    nt = seq // tt
    rb = lambda col: (lambda b, t: (b * nt + t, col))
    par = lambda b, t: (layer, 0, 0)
    return pl.pallas_call(
        functools.partial(_gdn_prompt_kernel, n_chunks=tt // CHUNK),
        grid=(batch, nt),
        in_specs=[
            pl.BlockSpec((tt, C_CONV), rb(COL_QKVB // C_CONV)),
            pl.BlockSpec((tt, W_B_V), rb(COL_ZB // W_B_V)),
            pl.BlockSpec((tt, LANES), rb(0)),
            pl.BlockSpec((None, CONV_W, C_CONV), par),
            pl.BlockSpec((None, SUBLANES, LANES), par),
            pl.BlockSpec((None, 1, DV_B), par),
        ],
        out_specs=[
            pl.BlockSpec((tt, W_B_V), rb(0)),
            pl.BlockSpec((1, H_B, DK_B, DV_B), lambda b, t: (b, 0, 0, 0)),
            pl.BlockSpec((1, CONV_W - 1, C_CONV), lambda b, t: (b, 0, 0)),
        ],
        out_shape=[
            jax.ShapeDtypeStruct((batch * seq, W_B_V), BF16),
            jax.ShapeDtypeStruct((batch, H_B, DK_B, DV_B), F32),
            jax.ShapeDtypeStruct((batch, CONV_W - 1, C_CONV), F32),
        ],
        scratch_shapes=[
            pltpu.VMEM((H_B, DK_B, DV_B), F32),
            pltpu.VMEM((tt + SUBLANES, C_CONV), F32),
            pltpu.VMEM((tt, C_CONV), F32),
        ],
        compiler_params=pltpu.CompilerParams(
            dimension_semantics=("parallel", "arbitrary"), vmem_limit_bytes=VMEM_LIMIT),
        name="gdn_prompt",
    )(proj_main, proj_main, proj_small, conv_w, prm, gnb)


def _gdn_sample_kernel(*refs, seg, aliased):
    x_ref, z_ref, sm_ref, cw_ref, prm_ref, gnb_ref, s0_ref, cb_ref = refs[:8]
    hb_ref, s_out, conv_out, xp_s, cv_s = refs[8 + aliased:]
    n = x_ref.shape[0]
    nb = n // seg
    ps, js = range(H_B // 2), range(nb)
    sl = [slice(j * seg, (j + 1) * seg) for j in js]
    stride = 2 * SUBLANES
    for j in js:
        base = j * stride + SUBLANES
        xp_s[base - SUBLANES:base, :] = jnp.zeros((SUBLANES, C_CONV), F32)
        xp_s[base - (CONV_W - 1):base, :] = cb_ref[j]
        xp_s[base:base + seg, :] = _f32(x_ref[sl[j], :])
        _conv_silu(xp_s, cw_ref, cv_s, base - (CONV_W - 1), j * seg, seg)
        conv_out[j] = xp_s[base + seg - (CONV_W - 1):base + seg, :]

    lower, _, _, _ = _seg_masks(n, seg)
    lower2, strict2, upper2, diag2 = _seg_masks(n, seg, reps=2)
    first = _iota((n, 2 * n), 1) < n
    masks = (lower2, strict2, diag2, first)
    n_sq = int(math.log2(seg)) - 1
    tabs = [_gate_tables(sm_ref[...], prm_ref[...], "gdn", _one_hot_bf(lower), _one_hot_bf(upper2), None)]
    load = lambda b, p, part: cv_s[:, part * W_B_QK + 2 * p * DK_B:part * W_B_QK + 2 * (p + 1) * DK_B]
    res = {}
    _run(_gdn_intra_gen(load, tabs, masks, n_sq, res))
    r, u = {}, {}
    for p in ps:
        q_dec, kn, w_v, w_k, qk2, gam_row = res[0, p]
        for e in range(2):
            h = 2 * p + e
            lhs = [jnp.concatenate([w_k[e][sl[j], :], q_dec[e][sl[j], :]], axis=0) for j in js]
            r[h] = [_mm(lhs[j], _bf(s0_ref[j, h])) for j in js]
    for p in ps:
        q_dec, kn, w_v, w_k, qk2, gam_row = res[0, p]
        for e in range(2):
            h = 2 * p + e
            u[h] = _bf(w_v[e] - jnp.concatenate([r[h][j][0:seg, :] for j in js], axis=0))
        oq = [jnp.concatenate([r[2 * p + e][j][seg:, :] for j in js], axis=0) for e in range(2)]
        o2 = jnp.concatenate(oq, axis=1) + _mm(qk2, _bd2(u[2 * p], u[2 * p + 1]))
        for e in range(2):
            h = 2 * p + e
            hb_ref[:, h * DV_B:(h + 1) * DV_B] = _bf(_gdn_out(
                o2[:, e * DV_B:(e + 1) * DV_B], gnb_ref[...], _f32(z_ref[:, h * DV_B:(h + 1) * DV_B])))
    for p in ps:
        q_dec, kn, w_v, w_k, qk2, gam_row = res[0, p]
        for e in range(2):
            h = 2 * p + e
            for j in js:
                g_l = gam_row[e][(j + 1) * seg - 1:(j + 1) * seg, :]
                k_dec = kn[e][sl[j], :] * jnp.exp(g_l - gam_row[e][sl[j], :])
                s_out[j, h] = s0_ref[j, h] * jnp.exp(g_l) + _mm_tn(_bf(k_dec), u[h][sl[j], :])


def _gdn_sample(proj_main, proj_small, conv_w, prm, gnb, s0, cb, s_stack, layer, batch, seq):
    depth = s0.shape[0]
    nb = CHUNK // seq
    n = nb * seq
    rb = lambda col: (lambda g: (g, col))
    par = lambda g: (layer, 0, 0)
    aliased = s_stack is not None
    in_specs = [
        pl.BlockSpec((n, C_CONV), rb(COL_QKVB // C_CONV)),
        pl.BlockSpec((n, W_B_V), rb(COL_ZB // W_B_V)),
        pl.BlockSpec((n, LANES), rb(0)),
        pl.BlockSpec((None, CONV_W, C_CONV), par),
        pl.BlockSpec((None, SUBLANES, LANES), par),
        pl.BlockSpec((None, 1, DV_B), par),
        pl.BlockSpec((None, nb, H_B, DK_B, DV_B), lambda g: (layer, g, 0, 0, 0)),
        pl.BlockSpec((None, nb, CONV_W - 1, C_CONV), lambda g: (layer, g, 0, 0)),
    ]
    args = [proj_main, proj_main, proj_small, conv_w, prm, gnb, s0, cb]
    if aliased:
        in_specs.append(pl.BlockSpec(memory_space=pl.ANY))
        args.append(s_stack)
    return pl.pallas_call(
        functools.partial(_gdn_sample_kernel, seg=seq, aliased=int(aliased)),
        grid=(batch // nb,),
        in_specs=in_specs,
        out_specs=[
            pl.BlockSpec((n, W_B_V), rb(0)),
            pl.BlockSpec((None, nb, H_B, DK_B, DV_B), lambda g: (layer, g, 0, 0, 0)),
            pl.BlockSpec((nb, CONV_W - 1, C_CONV), lambda g: (g, 0, 0)),
        ],
        out_shape=[
            jax.ShapeDtypeStruct((batch * seq, W_B_V), BF16),
            jax.ShapeDtypeStruct((depth, batch, H_B, DK_B, DV_B), F32),
            jax.ShapeDtypeStruct((batch, CONV_W - 1, C_CONV), F32),
        ],
        scratch_shapes=[
            pltpu.VMEM((nb * 2 * SUBLANES, C_CONV), F32),
            pltpu.VMEM((n, C_CONV), F32),
        ],
        input_output_aliases={8: 1} if aliased else {},
        compiler_params=pltpu.CompilerParams(
            dimension_semantics=("parallel",), vmem_limit_bytes=VMEM_LIMIT),
        name="gdn_sample",
    )(*args)


def _pack_w_in(w_in):
    w = w_in.astype(BF16)
    o = 0
    parts = {}
    for name, width in (("qa", W_A_QK), ("ka", W_A_QK), ("va", W_A_V), ("ip", H_A), ("fp", H_A), ("oa", W_A_V),
                        ("qkvb", C_CONV), ("ab", H_B), ("betab", H_B), ("zb", W_B_V), ("ga", D_MODEL),
                        ("gb", D_MODEL)):
        parts[name] = w[:, :, o:o + width]
        o += width
    main = jnp.concatenate([parts[k] for k in ("qa", "ka", "va", "oa", "qkvb", "zb", "ga", "gb")], axis=-1)
    small = jnp.concatenate([parts[k] for k in ("ip", "fp", "ab", "betab")], axis=-1)
    small = jnp.pad(small, ((0, 0), (0, 0), (0, LANES - small.shape[-1])))
    return main, small


def _param_rows(b_igate, b_fgate, dt_bias, a_log):
    bias = jnp.concatenate([b_igate, b_fgate, dt_bias], axis=-1)
    bias = jnp.pad(bias, ((0, 0), (0, LANES - bias.shape[-1])))
    alog = jnp.pad(a_log, ((0, 0), (SM_A, LANES - SM_A - H_B)))
    rows = jnp.stack([bias, alog], axis=1)
    return jnp.pad(rows, ((0, 0), (0, SUBLANES - 2), (0, 0))).astype(F32)


def kernel(x_prompt, x_sample, p_prompt, p_sample, state_mlstm_C, state_mlstm_n, state_mlstm_m, state_gdn_S,
           state_gdn_conv, g_mix, w_in, b_igate, b_fgate, g_norm_a, conv_w, a_log, dt_bias, g_norm_b, w_pa, w_pb,
           w_out, g_ffn, w_up, w_down, g_ple, w_pg, w_pp, g_final):
    depth = w_in.shape[0]
    bp, tp, _ = x_prompt.shape
    bs, ts, _ = x_sample.shape
    w_main, w_small = _pack_w_in(w_in)
    prm = _param_rows(b_igate, b_fgate, dt_bias, a_log)
    w = {k: v.astype(BF16) for k, v in (("w_pa", w_pa), ("w_pb", w_pb), ("w_out", w_out), ("w_up", w_up),
                                          ("w_down", w_down), ("w_pg", w_pg), ("w_pp", w_pp))}
    w["g_ffn"] = g_ffn.reshape(depth, 1, D_MODEL)
    w["g_ple"] = g_ple.reshape(depth, 1, D_MODEL)
    w["g_final"] = g_final.reshape(1, D_MODEL)
    g_in = g_mix.reshape(depth, 1, D_MODEL)
    gna = g_norm_a.reshape(depth, 1, W_A_V)
    gnb = g_norm_b.reshape(depth, 1, DV_B)
    pp = p_prompt.reshape(depth, bp * tp, PLE_DIM)
    ps = p_sample.reshape(depth, bs * ts, PLE_DIM)
    xp = x_prompt.reshape(bp * tp, D_MODEL)
    xs = x_sample.reshape(bs * ts, D_MODEL)
    outs = {k: [] for k in ("pC", "pn", "pm", "pS", "pc", "sn", "sm", "sc")}
    c_stack = s_stack = None
    for i in range(depth):
        final = i == depth - 1

        pj, pj_s = _in_proj(xp, g_in, w_main, w_small, i)
        ha, c1, n1, m1 = _mlstm_prompt(pj, pj_s, prm, gna, i, bp, tp)
        hb, s1, v1 = _gdn_prompt(pj, pj_s, conv_w, prm, gnb, i, bp, tp)
        xp = _post(xp, ha, hb, pj, pp, w, i, final)
        outs["pC"].append(c1)
        outs["pn"].append(n1)
        outs["pm"].append(m1[:, :H_A, 0])
        outs["pS"].append(s1)
        outs["pc"].append(v1)

        qj, qj_s = _in_proj(xs, g_in, w_main, w_small, i)
        ha, c_stack, n2, m2 = _mlstm_sample(qj, qj_s, prm, gna, state_mlstm_C, state_mlstm_n, state_mlstm_m,
                                            c_stack, i, bs, ts)
        hb, s_stack, v2 = _gdn_sample(qj, qj_s, conv_w, prm, gnb, state_gdn_S, state_gdn_conv, s_stack,
                                      i, bs, ts)
        xs = _post(xs, ha, hb, qj, ps, w, i, final)
        outs["sn"].append(n2)
        outs["sm"].append(m2)
        outs["sc"].append(v2)

    st = {k: jnp.stack(v) for k, v in outs.items()}
    return (xp.reshape(bp, tp, D_MODEL), xs.reshape(bs, ts, D_MODEL),
            st["pC"], st["pn"], st["pm"], st["pS"], st["pc"],
            c_stack, st["sn"], st["sm"], s_stack, st["sc"])
```

```python
import functools
import math

import jax
import jax.numpy as jnp
from jax import lax
from jax.experimental import pallas as pl
from jax.experimental.pallas import tpu as pltpu

F32 = jnp.float32
BF16 = jnp.bfloat16

D_MODEL = 1024
H_A, DK_A, DV_A = 4, 128, 256
H_B, DK_B, DV_B = 8, 128, 128
W_A_QK, W_A_V = H_A * DK_A, H_A * DV_A
W_B_QK, W_B_V = H_B * DK_B, H_B * DV_B
CONV_W = 4
C_CONV = 2 * W_B_QK + W_B_V
D_FF = 4 * D_MODEL
PLE_DIM = 256
CHUNK = 64
EPS = 1e-6

LANES = 128
SUBLANES = 8

COL_QA, COL_KA, COL_VA, COL_OA = 0, 512, 1024, 2048
COL_QKVB, COL_ZB, COL_GA, COL_GB = 3072, 6144, 7168, 8192
N_MAIN = 9216
SM_I, SM_F, SM_A, SM_BETA = 0, 4, 8, 16

TILE_T = 256
GRID_T = 1024
VMEM_LIMIT = 56 * 1024 * 1024


def _bf(x):
    return x.astype(BF16)


def _f32(x):
    return x.astype(F32)


def _mm(a, b):
    return jnp.dot(a, b, preferred_element_type=F32)


def _mm_nt(a, b):
    return lax.dot_general(a, b, (((1,), (1,)), ((), ())), preferred_element_type=F32)


def _mm_tn(a, b):
    return lax.dot_general(a, b, (((0,), (0,)), ((), ())), preferred_element_type=F32)


def _split3(x):
    hi = _bf(x)
    r = x - _f32(hi)
    mid = _bf(r)
    lo = _bf(r - _f32(mid))
    return hi, mid, lo


def _sigmoid(x):
    return 1.0 / (1.0 + jnp.exp(-x))


def _softplus(x):
    return jnp.maximum(x, 0.0) + jnp.log1p(jnp.exp(-jnp.abs(x)))


def _rmsnorm(x, g):
    return x * lax.rsqrt(jnp.mean(x * x, axis=-1, keepdims=True) + EPS) * g


def _iota(shape, dim):
    return lax.broadcasted_iota(jnp.int32, shape, dim)


def _seg_masks(n, seg, reps=1):
    row = _iota((n, reps * n), 0)
    col = _iota((n, reps * n), 1) % n
    same = True if seg == n else (row // seg) == (col // seg)
    lower = jnp.logical_and(col <= row, same)
    strict = jnp.logical_and(col < row, same)
    upper = jnp.logical_and(col >= row, same)
    return lower, strict, upper, row == col


def _one_hot_bf(mask):
    return jnp.where(mask, 1.0, 0.0).astype(BF16)


def _rr(*gens):
    active = list(gens)
    while active:
        for g in list(active):
            try:
                next(g)
            except StopIteration:
                active.remove(g)
        yield


def _run(*gens):
    for _ in _rr(*gens):
        pass


def _in_kernel(x_ref, g_ref, wm_ref, ws_ref, om_ref, os_ref, h_ref):
    @pl.when(pl.program_id(1) == 0)
    def _():
        h_ref[...] = _bf(_rmsnorm(x_ref[...], g_ref[...]))
        os_ref[...] = _mm(h_ref[...], ws_ref[...])

    om_ref[...] = _bf(_mm(h_ref[...], wm_ref[...]))


def _in_proj(x, g, w_main, w_small, layer):
    m = x.shape[0]
    tm = min(m, 2048)
    tn = 1024
    return pl.pallas_call(
        _in_kernel,
        grid=(m // tm, N_MAIN // tn),
        in_specs=[
            pl.BlockSpec((tm, D_MODEL), lambda i, j: (i, 0)),
            pl.BlockSpec((None, 1, D_MODEL), lambda i, j: (layer, 0, 0)),
            pl.BlockSpec((None, D_MODEL, tn), lambda i, j: (layer, 0, j)),
            pl.BlockSpec((None, D_MODEL, LANES), lambda i, j: (layer, 0, 0)),
        ],
        out_specs=[
            pl.BlockSpec((tm, tn), lambda i, j: (i, j)),
            pl.BlockSpec((tm, LANES), lambda i, j: (i, 0)),
        ],
        out_shape=[jax.ShapeDtypeStruct((m, N_MAIN), BF16), jax.ShapeDtypeStruct((m, LANES), F32)],
        scratch_shapes=[pltpu.VMEM((tm, D_MODEL), BF16)],
        compiler_params=pltpu.CompilerParams(
            dimension_semantics=("parallel", "arbitrary"), vmem_limit_bytes=VMEM_LIMIT),
        name="in_proj",
    )(x, g, w_main, w_small)


def _post_kernel(x_ref, ha_ref, hb_ref, ga_ref, gb_ref, p_ref, wpa_ref, wpb_ref, wout_ref, gffn_ref,
                 wup_ref, wdown_ref, gple_ref, wpg_ref, wpp_ref, gfin_ref, o_ref, *, final):
    x = x_ref[...]
    a = _mm(ha_ref[...], wpa_ref[...])
    b = _mm(hb_ref[...], wpb_ref[...])
    mix = _sigmoid(_f32(ga_ref[...])) * a + _sigmoid(_f32(gb_ref[...])) * b
    x = x + _mm(_bf(mix), wout_ref[...])
    hn = _bf(_rmsnorm(x, gffn_ref[...]))
    acc = jnp.zeros_like(x)
    fc = 1024
    for c in range(D_FF // fc):
        u = _mm(hn, wup_ref[:, c * fc:(c + 1) * fc])
        r = jnp.maximum(u, 0.0)
        acc = acc + _mm(_bf(r * r), wdown_ref[c * fc:(c + 1) * fc, :])
    x = x + acc
    hp = _bf(_rmsnorm(x, gple_ref[...]))
    gate = _sigmoid(_mm(hp, wpg_ref[...]))
    x = x + gate * _mm(_bf(p_ref[...]), wpp_ref[...])
    if final:
        x = _rmsnorm(x, gfin_ref[...])
    o_ref[...] = x


def _post(x, ha, hb, proj_main, p, w, layer, final):
    m = x.shape[0]
    tm = min(m, 512)
    row = lambda i: (i, 0)
    wspec = lambda r, c: pl.BlockSpec((None, r, c), lambda i: (layer, 0, 0), pipeline_mode=pl.Buffered(1))
    return pl.pallas_call(
        functools.partial(_post_kernel, final=final),
        grid=(m // tm,),
        in_specs=[
            pl.BlockSpec((tm, D_MODEL), row),
            pl.BlockSpec((tm, W_A_V), row),
            pl.BlockSpec((tm, W_B_V), row),
            pl.BlockSpec((tm, D_MODEL), lambda i: (i, COL_GA // D_MODEL)),
            pl.BlockSpec((tm, D_MODEL), lambda i: (i, COL_GB // D_MODEL)),
            pl.BlockSpec((None, tm, PLE_DIM), lambda i: (layer, i, 0)),
            wspec(W_A_V, D_MODEL),
            wspec(W_B_V, D_MODEL),
            wspec(D_MODEL, D_MODEL),
            wspec(1, D_MODEL),
            wspec(D_MODEL, D_FF),
            wspec(D_FF, D_MODEL),
            wspec(1, D_MODEL),
            wspec(D_MODEL, D_MODEL),
            wspec(PLE_DIM, D_MODEL),
            pl.BlockSpec((1, D_MODEL), lambda i: (0, 0), pipeline_mode=pl.Buffered(1)),
        ],
        out_specs=pl.BlockSpec((tm, D_MODEL), row),
        out_shape=jax.ShapeDtypeStruct((m, D_MODEL), F32),
        compiler_params=pltpu.CompilerParams(
            dimension_semantics=("parallel",), vmem_limit_bytes=VMEM_LIMIT),
        name="post_final" if final else "post",
    )(x, ha, hb, proj_main, proj_main, p, w["w_pa"], w["w_pb"], w["w_out"], w["g_ffn"], w["w_up"],
      w["w_down"], w["g_ple"], w["w_pg"], w["w_pp"], w["g_final"])


def _gate_tables(small, prm, kind, lower_bf, upper_bf, eye_bf):
    n = small.shape[0]
    lane = _iota((n, LANES), 1)
    p = small + prm[0:1, :]
    if kind == "mlstm":
        lf = -_softplus(-p)
        pm = jnp.where(lane < SM_F, p, jnp.where(lane < SM_A, lf, 0.0))
    else:
        g = -jnp.exp(prm[1:2, :]) * _softplus(p)
        beta = _sigmoid(p)
        in_a = jnp.logical_and(lane >= SM_A, lane < SM_BETA)
        in_b = jnp.logical_and(lane >= SM_BETA, lane < SM_BETA + H_B)
        pm = jnp.where(in_a, g, jnp.where(in_b, beta, 0.0))
    parts = _split3(pm)
    cs = sum(_mm(lower_bf, q) for q in parts)
    pm_t = sum(_mm_tn(q, eye_bf) for q in parts) if eye_bf is not None else None
    cs_t = sum(_mm_tn(q, upper_bf) for q in parts)
    return pm, cs, pm_t, cs_t


def _mlstm_out(num, nq, m_row, gna, o):
    den = jnp.maximum(jnp.abs(nq), jnp.exp(-m_row))
    hh = num / den
    hr = hh * lax.rsqrt(jnp.mean(hh * hh, axis=1, keepdims=True) + EPS)
    return hr * gna * _sigmoid(o)


def _mlstm_tile_gen(q_ref, k_ref, v_ref, o_ref, sm_ref, gna_ref, ha_ref, prm, ch, nh, mh, n_chunks, rows_of):
    L = CHUNK
    hs = range(H_A)
    lower, _, upper, diag = _seg_masks(L, L)
    lower_bf, upper_bf, eye_bf = _one_hot_bf(lower), _one_hot_bf(upper), _one_hot_bf(diag)
    scale = DK_A ** -0.5
    for c in range(n_chunks):
        rows = rows_of(c)
        pm, cs, pm_t, cs_t = _gate_tables(sm_ref[rows, :], prm, "mlstm", lower_bf, upper_bf, eye_bf)
        qb = [q_ref[rows, h * DK_A:(h + 1) * DK_A] for h in hs]
        kb = [k_ref[rows, h * DK_A:(h + 1) * DK_A] for h in hs]
        vb = [v_ref[rows, h * DV_A:(h + 1) * DV_A] for h in hs]
        qk = [_mm_nt(qb[h], kb[h]) for h in hs]
        yield
        b_row, a_row, s_loc, kw_loc = [], [], [], []
        for h in hs:
            i_row = pm[:, SM_I + h:SM_I + h + 1]
            b_row.append(cs[:, SM_F + h:SM_F + h + 1])
            dlog = jnp.where(
                lower, b_row[h] - cs_t[SM_F + h:SM_F + h + 1, :] + pm_t[SM_I + h:SM_I + h + 1, :], -jnp.inf)
            a_row.append(jnp.max(dlog, axis=1, keepdims=True))
            s_loc.append(qk[h] * (scale * jnp.exp(dlog - a_row[h])))
            b_last = b_row[h][L - 1:L, :]
            kw_loc.append(_f32(kb[h]) * jnp.exp(i_row + b_last - b_row[h] - a_row[h][L - 1:L, :]))
        sv = [_mm(_bf(s_loc[h]), vb[h]) for h in hs]
        kv = [_mm_tn(_bf(kw_loc[h]), vb[h]) for h in hs]
        qc = [_mm(qb[h], _bf(ch[h])) for h in hs]
        yield
        for h in hs:
            m_row = jnp.maximum(b_row[h] + mh[h], a_row[h])
            w_inter = scale * jnp.exp(b_row[h] + mh[h] - m_row)
            f_intra = jnp.exp(a_row[h] - m_row)
            num = w_inter * qc[h] + f_intra * sv[h]
            nq = (w_inter * jnp.sum(_f32(qb[h]) * nh[h], axis=1, keepdims=True)
                  + f_intra * jnp.sum(s_loc[h], axis=1, keepdims=True))
            ha_ref[rows, h * DV_A:(h + 1) * DV_A] = _bf(_mlstm_out(
                num, nq, m_row, gna_ref[:, h * DV_A:(h + 1) * DV_A], _f32(o_ref[rows, h * DV_A:(h + 1) * DV_A])))
            m_new = m_row[L - 1:L, :]
            dec = jnp.exp(b_row[h][L - 1:L, :] + mh[h] - m_new)
            g_loc = jnp.exp(a_row[h][L - 1:L, :] - m_new)
            ch[h] = dec * ch[h] + g_loc * kv[h]
            nh[h] = dec * nh[h] + g_loc * jnp.sum(kw_loc[h], axis=0, keepdims=True)
            mh[h] = m_new
        yield


def _rep_rows(x, seg):
    g = x.shape[0]
    return jnp.concatenate([jnp.broadcast_to(x[j:j + 1, :], (seg, x.shape[1])) for j in range(g)], axis=0)


def _mlstm_sample_kernel(*refs, seg, aliased):
    (q_ref, k_ref, v_ref, o_ref, sm_ref, prm_ref, gna_ref, c0_ref, n0_ref, m0_ref) = refs[:10]
    ha_ref, c_out, n_out, m_out = refs[10 + aliased:]
    n = q_ref.shape[0]
    nb = n // seg
    hs, js = range(H_A), range(nb)
    sl = [slice(j * seg, (j + 1) * seg) for j in js]
    last = [slice((j + 1) * seg - 1, (j + 1) * seg) for j in js]
    lower, _, upper, diag = _seg_masks(n, seg)
    lower_bf, upper_bf, eye_bf = _one_hot_bf(lower), _one_hot_bf(upper), _one_hot_bf(diag)
    scale = DK_A ** -0.5
    pm, cs, pm_t, cs_t = _gate_tables(sm_ref[...], prm_ref[...], "mlstm", lower_bf, upper_bf, eye_bf)
    m_rep = _rep_rows(m0_ref[...], seg)
    qb = [q_ref[:, h * DK_A:(h + 1) * DK_A] for h in hs]
    kb = [k_ref[:, h * DK_A:(h + 1) * DK_A] for h in hs]
    vb = [v_ref[:, h * DV_A:(h + 1) * DV_A] for h in hs]
    i_row = [pm[:, SM_I + h:SM_I + h + 1] for h in hs]
    b_row = [cs[:, SM_F + h:SM_F + h + 1] for h in hs]
    qk = [_mm_nt(qb[h], kb[h]) for h in hs]
    qc = [jnp.concatenate([_mm(qb[h][sl[j], :], _bf(c0_ref[j, h])) for j in js], axis=0) for h in hs]
    m_row, w_inter, s = [], [], []
    for h in hs:
        dlog = jnp.where(lower, b_row[h] - cs_t[SM_F + h:SM_F + h + 1, :] + pm_t[SM_I + h:SM_I + h + 1, :],
                         -jnp.inf)
        inter = b_row[h] + m_rep[:, h:h + 1]
        m_row.append(jnp.maximum(inter, jnp.max(dlog, axis=1, keepdims=True)))
        w_inter.append(scale * jnp.exp(inter - m_row[h]))
        s.append(qk[h] * (scale * jnp.exp(dlog - m_row[h])))
    sv = [_mm(_bf(s[h]), vb[h]) for h in hs]
    for h in hs:
        n_rep = _rep_rows(n0_ref[:, h, :], seg)
        num = w_inter[h] * qc[h] + sv[h]
        nq = (w_inter[h] * jnp.sum(_f32(qb[h]) * n_rep, axis=1, keepdims=True)
              + jnp.sum(s[h], axis=1, keepdims=True))
        ha_ref[:, h * DV_A:(h + 1) * DV_A] = _bf(_mlstm_out(
            num, nq, m_row[h], gna_ref[:, h * DV_A:(h + 1) * DV_A], _f32(o_ref[:, h * DV_A:(h + 1) * DV_A])))
    for h in hs:
        kf = _f32(kb[h])
        for j in js:
            m_new = m_row[h][last[j], :]
            b_last = b_row[h][last[j], :]
            dec = jnp.exp(b_last + m0_ref[j:j + 1, h:h + 1] - m_new)
            kw = kf[sl[j], :] * jnp.exp(i_row[h][sl[j], :] + b_last - b_row[h][sl[j], :] - m_new)
            c_out[j, h] = dec * c0_ref[j, h] + _mm_tn(_bf(kw), vb[h][sl[j], :])
            n_out[j, h:h + 1, :] = dec * n0_ref[j, h:h + 1, :] + jnp.sum(kw, axis=0, keepdims=True)
            m_out[j:j + 1, h:h + 1] = m_new


def _mlstm_sample(proj_main, proj_small, prm, gna, c0, n0, m0, c_stack, layer, batch, seq):
    depth = c0.shape[0]
    nb = CHUNK // seq
    n = nb * seq
    rb = lambda col: (lambda g: (g, col))
    par = lambda g: (layer, 0, 0)
    aliased = c_stack is not None
    in_specs = [
        pl.BlockSpec((n, W_A_QK), rb(COL_QA // W_A_QK)),
        pl.BlockSpec((n, W_A_QK), rb(COL_KA // W_A_QK)),
        pl.BlockSpec((n, W_A_V), rb(COL_VA // W_A_V)),
        pl.BlockSpec((n, W_A_V), rb(COL_OA // W_A_V)),
        pl.BlockSpec((n, LANES), rb(0)),
        pl.BlockSpec((None, SUBLANES, LANES), par),
        pl.BlockSpec((None, 1, W_A_V), par),
        pl.BlockSpec((None, nb, H_A, DK_A, DV_A), lambda g: (layer, g, 0, 0, 0)),
        pl.BlockSpec((None, nb, H_A, DK_A), lambda g: (layer, g, 0, 0)),
        pl.BlockSpec((None, nb, H_A), lambda g: (layer, g, 0)),
    ]
    args = [proj_main, proj_main, proj_main, proj_main, proj_small, prm, gna, c0, n0, m0]
    if aliased:
        in_specs.append(pl.BlockSpec(memory_space=pl.ANY))
        args.append(c_stack)
    return pl.pallas_call(
        functools.partial(_mlstm_sample_kernel, seg=seq, aliased=int(aliased)),
        grid=(batch // nb,),
        in_specs=in_specs,
        out_specs=[
            pl.BlockSpec((n, W_A_V), rb(0)),
            pl.BlockSpec((None, nb, H_A, DK_A, DV_A), lambda g: (layer, g, 0, 0, 0)),
            pl.BlockSpec((nb, H_A, DK_A), lambda g: (g, 0, 0)),
            pl.BlockSpec((nb, H_A), lambda g: (g, 0)),
        ],
        out_shape=[
            jax.ShapeDtypeStruct((batch * seq, W_A_V), BF16),
            jax.ShapeDtypeStruct((depth, batch, H_A, DK_A, DV_A), F32),
            jax.ShapeDtypeStruct((batch, H_A, DK_A), F32),
            jax.ShapeDtypeStruct((batch, H_A), F32),
        ],
        input_output_aliases={10: 1} if aliased else {},
        compiler_params=pltpu.CompilerParams(
            dimension_semantics=("parallel",), vmem_limit_bytes=VMEM_LIMIT),
        name="mlstm_sample",
    )(*args)


def _conv_silu(xp_ref, w_ref, cv_ref, src0, dst0, n_rows):
    cb = 512
    top = src0 + CONV_W - 1 - SUBLANES
    for c0 in range(0, C_CONV, cb):
        cols = slice(c0, c0 + cb)
        xx = xp_ref[top:top + SUBLANES + n_rows, cols]
        acc = w_ref[CONV_W - 1:CONV_W, cols] * xx[SUBLANES:, :]
        for j in range(1, CONV_W):
            acc = acc + w_ref[CONV_W - 1 - j:CONV_W - j, cols] * pltpu.roll(xx, j, axis=0)[SUBLANES:, :]
        cv_ref[dst0:dst0 + n_rows, cols] = acc * _sigmoid(acc)


def _bd_rhs(x_f, first):
    return jnp.concatenate([_bf(jnp.where(first, x_f, 0.0)), _bf(jnp.where(first, 0.0, x_f))], axis=0)


def _pair_split(x2, first, want_lhs=True, want_rhs=True):
    hi = _bf(x2)
    hi_f = _f32(hi)
    lo_f = x2 - hi_f
    lhs3 = jnp.concatenate([hi, hi, _bf(lo_f)], axis=1) if want_lhs else None
    rhs3 = None
    if want_rhs:
        bd_hi = _bd_rhs(hi_f, first)
        rhs3 = jnp.concatenate([bd_hi, _bd_rhs(lo_f, first), bd_hi], axis=0)
    return lhs3, rhs3


def _bd2(a, b):
    z = jnp.zeros_like(a)
    return jnp.concatenate([jnp.concatenate([a, z], axis=1), jnp.concatenate([z, b], axis=1)], axis=0)


def _gdn_intra_gen(load_cv, tabs, masks, n_sq, res):
    lower2, strict2, diag2, first = masks
    items = [(b, p) for b in range(len(tabs)) for p in range(H_B // 2)]
    gam_row, decay2, eg, qn, kn, kbv, rhs, kq = {}, {}, {}, {}, {}, {}, {}, {}
    for b, p in items:
        pm, cs, _, cs_t2 = tabs[b]
        hh = (2 * p, 2 * p + 1)
        gam_row[b, p] = [cs[:, SM_A + h:SM_A + h + 1] for h in hh]
        beta_row = [pm[:, SM_BETA + h:SM_BETA + h + 1] for h in hh]
        gam_lane2 = jnp.where(first[0:1, :], cs_t2[SM_A + hh[0]:SM_A + hh[0] + 1, :],
                              cs_t2[SM_A + hh[1]:SM_A + hh[1] + 1, :])
        gam_row2 = jnp.where(first, gam_row[b, p][0], gam_row[b, p][1])
        decay2[b, p] = jnp.exp(jnp.where(lower2, gam_row2 - gam_lane2, -jnp.inf))
        eg[b, p] = [jnp.exp(g) for g in gam_row[b, p]]
        cq2, ck2, vv2 = load_cv(b, p, 0), load_cv(b, p, 1), load_cv(b, p, 2)
        qn[b, p], kn[b, p], kbv[b, p], rhs[b, p] = [], [], [], []
        for e in range(2):
            cq, ck, vv = (x[:, e * DK_B:(e + 1) * DK_B] for x in (cq2, ck2, vv2))
            qn[b, p].append(cq * (lax.rsqrt(jnp.sum(cq * cq, axis=1, keepdims=True) + EPS) * (DK_B ** -0.5)))
            kn[b, p].append(ck * lax.rsqrt(jnp.sum(ck * ck, axis=1, keepdims=True) + EPS))
            kbv[b, p].append(kn[b, p][e] * beta_row[e])
            rhs[b, p].append(jnp.concatenate([vv * beta_row[e], kbv[b, p][e] * eg[b, p][e]], axis=1))
        lhs = jnp.concatenate([_bf(jnp.concatenate(kbv[b, p], axis=1)), _bf(jnp.concatenate(qn[b, p], axis=1))],
                              axis=0)
        kq[b, p] = _mm_nt(lhs, _bd2(_bf(kn[b, p][0]), _bf(kn[b, p][1])))
    yield
    n = lower2.shape[0]
    eye2 = jnp.where(diag2, 1.0, 0.0)
    pw, tinv, l_pw, r_pw = {}, {}, {}, {}
    for it in items:
        pw[it] = jnp.where(strict2, -(kq[it][0:n, :] * decay2[it]), 0.0)
        tinv[it] = eye2 + pw[it]
        l_pw[it], r_pw[it] = _pair_split(pw[it], first)
    for step in range(n_sq):
        for it in items:
            pw[it] = _mm(l_pw[it], r_pw[it])
        yield
        for it in items:
            l_pw[it], r_pw[it] = _pair_split(pw[it], first, want_lhs=step + 1 < n_sq)
            tinv[it] = tinv[it] + _mm(_pair_split(tinv[it], first, want_rhs=False)[0], r_pw[it])
        yield
    sol = {}
    for it in items:
        parts = []
        for e in range(2):
            hi = _bf(rhs[it][e])
            parts.append((hi, _bf(rhs[it][e] - _f32(hi))))
        bd_hi = _bd2(parts[0][0], parts[1][0])
        rhs3 = jnp.concatenate([bd_hi, _bd2(parts[0][1], parts[1][1]), bd_hi], axis=0)
        sol[it] = _mm(_pair_split(tinv[it], first, want_rhs=False)[0], rhs3)
    yield
    for it in items:
        w_v = [sol[it][:, e * 2 * DV_B:e * 2 * DV_B + DV_B] for e in range(2)]
        w_k = [_bf(sol[it][:, e * 2 * DV_B + DV_B:(e + 1) * 2 * DV_B]) for e in range(2)]
        q_dec = [_bf(qn[it][e] * eg[it][e]) for e in range(2)]
        res[it] = (q_dec, kn[it], w_v, w_k, _bf(kq[it][n:, :] * decay2[it]), gam_row[it])


def _gdn_scan_pair(res_it, s_pair, g_l):
    q_dec, kn, w_v, w_k, qk2, gam_row = res_it
    n = w_v[0].shape[0]
    r = [_mm(jnp.concatenate([w_k[e], q_dec[e]], axis=0), _bf(s_pair[e])) for e in range(2)]
    yield
    u = [_bf(w_v[e] - r[e][0:n, :]) for e in range(2)]
    o2 = jnp.concatenate([r[0][n:, :], r[1][n:, :]], axis=1) + _mm(qk2, _bd2(u[0], u[1]))
    for e in range(2):
        k_dec = _bf(kn[e] * jnp.exp(g_l[e] - gam_row[e]))
        s_pair[e] = s_pair[e] * jnp.exp(g_l[e]) + _mm_tn(k_dec, u[e])
    yield
    return o2


def _gdn_out(o, gnb, z):
    return o * lax.rsqrt(jnp.mean(o * o, axis=1, keepdims=True) + EPS) * gnb * (z * _sigmoid(z))


def _gdn_tile_gen(z_ref, sm_ref, gnb_ref, hb_ref, cv_s, prm, state, n_chunks, io_rows):
    L = CHUNK
    ps = range(H_B // 2)
    rows = [slice(c * L, (c + 1) * L) for c in range(n_chunks)]
    io = [io_rows(c) for c in range(n_chunks)]
    lower, _, _, _ = _seg_masks(L, L)
    lower2, strict2, upper2, diag2 = _seg_masks(L, L, reps=2)
    first = _iota((L, 2 * L), 1) < L
    masks = (lower2, strict2, diag2, first)
    lower_bf, upper2_bf = _one_hot_bf(lower), _one_hot_bf(upper2)
    n_sq = int(math.log2(L)) - 1
    res, outs = {}, {}

    def intra(chunks):
        tabs = [_gate_tables(sm_ref[io[c], :], prm, "gdn", lower_bf, upper2_bf, None) for c in chunks]
        load = lambda b, p, part: _f32(cv_s[rows[chunks[b]],
                                            part * W_B_QK + 2 * p * DK_B:part * W_B_QK + 2 * (p + 1) * DK_B])
        sub = {}
        yield from _gdn_intra_gen(load, tabs, masks, n_sq, sub)
        for (b, p), v in sub.items():
            res[chunks[b], p] = v

    def scan(chunks):
        for c in chunks:
            gens = []
            for p in ps:
                g_l = [g[L - 1:L, :] for g in res[c, p][5]]
                gens.append(_gdn_scan_pair(res[c, p], state[p], g_l))
            for stage in range(2):
                for g in gens:
                    next(g)
                yield
            for p, g in zip(ps, gens):
                try:
                    next(g)
                except StopIteration as done:
                    outs[c, p] = done.value

    def emit(chunks):
        for c in chunks:
            for p in ps:
                for e in range(2):
                    h = 2 * p + e
                    hb_ref[io[c], h * DV_B:(h + 1) * DV_B] = _bf(_gdn_out(
                        outs[c, p][:, e * DV_B:(e + 1) * DV_B], gnb_ref[...],
                        _f32(z_ref[io[c], h * DV_B:(h + 1) * DV_B])))

    half = n_chunks // 2
    first_half, second_half = list(range(half)), list(range(half, n_chunks))
    yield from intra(first_half)
    yield from _rr(intra(second_half), scan(first_half))
    emit(first_half)
    yield from scan(second_half)
    emit(second_half)


def _mlstm_prompt_kernel(q_ref, k_ref, v_ref, o_ref, sm_ref, prm_ref, gna_ref,
                         ha_ref, c_out, n_out, m_out, c_s, n_s, m_s, *, n_chunks):
    t = pl.program_id(1)
    hs = range(H_A)

    @pl.when(t == 0)
    def _():
        c_s[...] = jnp.zeros_like(c_s)
        n_s[...] = jnp.zeros_like(n_s)
        m_s[...] = jnp.zeros_like(m_s)

    prm = prm_ref[...]

    def chunk(c, carry):
        ch = [c_s[h] for h in hs]
        nh = [n_s[h:h + 1, :] for h in hs]
        mh = [m_s[h:h + 1, 0:1] for h in hs]
        rows = pl.ds(pl.multiple_of(c * CHUNK, CHUNK), CHUNK)
        _run(_mlstm_tile_gen(q_ref, k_ref, v_ref, o_ref, sm_ref, gna_ref, ha_ref, prm, ch, nh, mh, 1,
                             lambda _: rows))
        for h in hs:
            c_s[h] = ch[h]
            n_s[h:h + 1, :] = nh[h]
            m_s[h:h + 1, :] = jnp.broadcast_to(mh[h], (1, LANES))
        return carry

    lax.fori_loop(0, n_chunks, chunk, 0)

    @pl.when(t == pl.num_programs(1) - 1)
    def _():
        c_out[0] = c_s[...]
        n_out[0] = n_s[0:H_A, :]
        m_out[0] = m_s[...]


def _mlstm_prompt(proj_main, proj_small, prm, gna, layer, batch, seq):
    tt = min(seq, GRID_T)
    nt = seq // tt
    rb = lambda col: (lambda b, t: (b * nt + t, col))
    par = lambda b, t: (layer, 0, 0)
    return pl.pallas_call(
        functools.partial(_mlstm_prompt_kernel, n_chunks=tt // CHUNK),
        grid=(batch, nt),
        in_specs=[
            pl.BlockSpec((tt, W_A_QK), rb(COL_QA // W_A_QK)),
            pl.BlockSpec((tt, W_A_QK), rb(COL_KA // W_A_QK)),
            pl.BlockSpec((tt, W_A_V), rb(COL_VA // W_A_V)),
            pl.BlockSpec((tt, W_A_V), rb(COL_OA // W_A_V)),
            pl.BlockSpec((tt, LANES), rb(0)),
            pl.BlockSpec((None, SUBLANES, LANES), par),
            pl.BlockSpec((None, 1, W_A_V), par),
        ],
        out_specs=[
            pl.BlockSpec((tt, W_A_V), rb(0)),
            pl.BlockSpec((1, H_A, DK_A, DV_A), lambda b, t: (b, 0, 0, 0)),
            pl.BlockSpec((1, H_A, DK_A), lambda b, t: (b, 0, 0)),
            pl.BlockSpec((1, SUBLANES, LANES), lambda b, t: (b, 0, 0)),
        ],
        out_shape=[
            jax.ShapeDtypeStruct((batch * seq, W_A_V), BF16),
            jax.ShapeDtypeStruct((batch, H_A, DK_A, DV_A), F32),
            jax.ShapeDtypeStruct((batch, H_A, DK_A), F32),
            jax.ShapeDtypeStruct((batch, SUBLANES, LANES), F32),
        ],
        scratch_shapes=[
            pltpu.VMEM((H_A, DK_A, DV_A), F32),
            pltpu.VMEM((SUBLANES, LANES), F32),
            pltpu.VMEM((SUBLANES, LANES), F32),
        ],
        compiler_params=pltpu.CompilerParams(
            dimension_semantics=("parallel", "arbitrary"), vmem_limit_bytes=VMEM_LIMIT),
        name="mlstm_prompt",
    )(proj_main, proj_main, proj_main, proj_main, proj_small, prm, gna)


def _gdn_prompt_kernel(x_ref, z_ref, sm_ref, cw_ref, prm_ref, gnb_ref,
                       hb_ref, s_out, conv_out, s_s, xp_s, cv_s, *, n_tiles, n_chunks):
    t = pl.program_id(1)
    L = CHUNK
    tt = n_chunks * L
    ps = range(H_B // 2)

    @pl.when(t == 0)
    def _():
        s_s[...] = jnp.zeros_like(s_s)
        xp_s[0:SUBLANES, :] = jnp.zeros((SUBLANES, C_CONV), F32)

    prm = prm_ref[...]

    def tile(ti, carry):
        base = pl.multiple_of(ti * tt, tt)
        xp_s[SUBLANES:SUBLANES + tt, :] = _f32(x_ref[pl.ds(base, tt), :])
        for c in range(n_chunks):
            _conv_silu(xp_s, cw_ref, cv_s, SUBLANES - (CONV_W - 1) + c * L, c * L, L)
        state = [[s_s[2 * p], s_s[2 * p + 1]] for p in ps]
        _run(_gdn_tile_gen(z_ref, sm_ref, gnb_ref, hb_ref, cv_s, prm, state, n_chunks,
                           lambda c: pl.ds(pl.multiple_of(base + c * L, L), L)))
        for p in ps:
            s_s[2 * p] = state[p][0]
            s_s[2 * p + 1] = state[p][1]
        xp_s[0:SUBLANES, :] = xp_s[tt:tt + SUBLANES, :]
        return carry

    lax.fori_loop(0, n_tiles, tile, 0)

    @pl.when(t == pl.num_programs(1) - 1)
    def _():
        s_out[0] = s_s[...]
        conv_out[0] = xp_s[SUBLANES - (CONV_W - 1):SUBLANES, :]


def _gdn_prompt(proj_main, proj_small, conv_w, prm, gnb, layer, batch, seq):
    tt = TILE_T
    blk = min(seq, GRID_T)
    nt = seq // blk
    rb = lambda col: (lambda b, t: (b * nt + t, col))
    par = lambda b, t: (layer, 0, 0)
    return pl.pallas_call(
        functools.partial(_gdn_prompt_kernel, n_tiles=blk // tt, n_chunks=tt // CHUNK),
        grid=(batch, nt),
        in_specs=[
            pl.BlockSpec((blk, C_CONV), rb(COL_QKVB // C_CONV)),
            pl.BlockSpec((blk, W_B_V), rb(COL_ZB // W_B_V)),
            pl.BlockSpec((blk, LANES), rb(0)),
            pl.BlockSpec((None, CONV_W, C_CONV), par),
            pl.BlockSpec((None, SUBLANES, LANES), par),
            pl.BlockSpec((None, 1, DV_B), par),
        ],
        out_specs=[
            pl.BlockSpec((blk, W_B_V), rb(0)),
            pl.BlockSpec((1, H_B, DK_B, DV_B), lambda b, t: (b, 0, 0, 0)),
            pl.BlockSpec((1, CONV_W - 1, C_CONV), lambda b, t: (b, 0, 0)),
        ],
        out_shape=[
            jax.ShapeDtypeStruct((batch * seq, W_B_V), BF16),
            jax.ShapeDtypeStruct((batch, H_B, DK_B, DV_B), F32),
            jax.ShapeDtypeStruct((batch, CONV_W - 1, C_CONV), F32),
        ],
        scratch_shapes=[
            pltpu.VMEM((H_B, DK_B, DV_B), F32),
            pltpu.VMEM((tt + SUBLANES, C_CONV), F32),
            pltpu.VMEM((tt, C_CONV), F32),
        ],
        compiler_params=pltpu.CompilerParams(
            dimension_semantics=("parallel", "arbitrary"), vmem_limit_bytes=VMEM_LIMIT),
        name="gdn_prompt",
    )(proj_main, proj_main, proj_small, conv_w, prm, gnb)


def _gdn_sample_kernel(*refs, seg, aliased):
    x_ref, z_ref, sm_ref, cw_ref, prm_ref, gnb_ref, s0_ref, cb_ref = refs[:8]
    hb_ref, s_out, conv_out, xp_s, cv_s = refs[8 + aliased:]
    n = x_ref.shape[0]
    nb = n // seg
    ps, js = range(H_B // 2), range(nb)
    sl = [slice(j * seg, (j + 1) * seg) for j in js]
    stride = 2 * SUBLANES
    for j in js:
        base = j * stride + SUBLANES
        xp_s[base - SUBLANES:base, :] = jnp.zeros((SUBLANES, C_CONV), F32)
        xp_s[base - (CONV_W - 1):base, :] = cb_ref[j]
        xp_s[base:base + seg, :] = _f32(x_ref[sl[j], :])
        _conv_silu(xp_s, cw_ref, cv_s, base - (CONV_W - 1), j * seg, seg)
        conv_out[j] = xp_s[base + seg - (CONV_W - 1):base + seg, :]

    lower, _, _, _ = _seg_masks(n, seg)
    lower2, strict2, upper2, diag2 = _seg_masks(n, seg, reps=2)
    first = _iota((n, 2 * n), 1) < n
    masks = (lower2, strict2, diag2, first)
    n_sq = int(math.log2(seg)) - 1
    tabs = [_gate_tables(sm_ref[...], prm_ref[...], "gdn", _one_hot_bf(lower), _one_hot_bf(upper2), None)]
    load = lambda b, p, part: cv_s[:, part * W_B_QK + 2 * p * DK_B:part * W_B_QK + 2 * (p + 1) * DK_B]
    res = {}
    _run(_gdn_intra_gen(load, tabs, masks, n_sq, res))
    r, u = {}, {}
    for p in ps:
        q_dec, kn, w_v, w_k, qk2, gam_row = res[0, p]
        for e in range(2):
            h = 2 * p + e
            lhs = [jnp.concatenate([w_k[e][sl[j], :], q_dec[e][sl[j], :]], axis=0) for j in js]
            r[h] = [_mm(lhs[j], _bf(s0_ref[j, h])) for j in js]
    for p in ps:
        q_dec, kn, w_v, w_k, qk2, gam_row = res[0, p]
        for e in range(2):
            h = 2 * p + e
            u[h] = _bf(w_v[e] - jnp.concatenate([r[h][j][0:seg, :] for j in js], axis=0))
        oq = [jnp.concatenate([r[2 * p + e][j][seg:, :] for j in js], axis=0) for e in range(2)]
        o2 = jnp.concatenate(oq, axis=1) + _mm(qk2, _bd2(u[2 * p], u[2 * p + 1]))
        for e in range(2):
            h = 2 * p + e
            hb_ref[:, h * DV_B:(h + 1) * DV_B] = _bf(_gdn_out(
                o2[:, e * DV_B:(e + 1) * DV_B], gnb_ref[...], _f32(z_ref[:, h * DV_B:(h + 1) * DV_B])))
    for p in ps:
        q_dec, kn, w_v, w_k, qk2, gam_row = res[0, p]
        for e in range(2):
            h = 2 * p + e
            for j in js:
                g_l = gam_row[e][(j + 1) * seg - 1:(j + 1) * seg, :]
                k_dec = kn[e][sl[j], :] * jnp.exp(g_l - gam_row[e][sl[j], :])
                s_out[j, h] = s0_ref[j, h] * jnp.exp(g_l) + _mm_tn(_bf(k_dec), u[h][sl[j], :])


def _gdn_sample(proj_main, proj_small, conv_w, prm, gnb, s0, cb, s_stack, layer, batch, seq):
    depth = s0.shape[0]
    nb = CHUNK // seq
    n = nb * seq
    rb = lambda col: (lambda g: (g, col))
    par = lambda g: (layer, 0, 0)
    aliased = s_stack is not None
    in_specs = [
        pl.BlockSpec((n, C_CONV), rb(COL_QKVB // C_CONV)),
        pl.BlockSpec((n, W_B_V), rb(COL_ZB // W_B_V)),
        pl.BlockSpec((n, LANES), rb(0)),
        pl.BlockSpec((None, CONV_W, C_CONV), par),
        pl.BlockSpec((None, SUBLANES, LANES), par),
        pl.BlockSpec((None, 1, DV_B), par),
        pl.BlockSpec((None, nb, H_B, DK_B, DV_B), lambda g: (layer, g, 0, 0, 0)),
        pl.BlockSpec((None, nb, CONV_W - 1, C_CONV), lambda g: (layer, g, 0, 0)),
    ]
    args = [proj_main, proj_main, proj_small, conv_w, prm, gnb, s0, cb]
    if aliased:
        in_specs.append(pl.BlockSpec(memory_space=pl.ANY))
        args.append(s_stack)
    return pl.pallas_call(
        functools.partial(_gdn_sample_kernel, seg=seq, aliased=int(aliased)),
        grid=(batch // nb,),
        in_specs=in_specs,
        out_specs=[
            pl.BlockSpec((n, W_B_V), rb(0)),
            pl.BlockSpec((None, nb, H_B, DK_B, DV_B), lambda g: (layer, g, 0, 0, 0)),
            pl.BlockSpec((nb, CONV_W - 1, C_CONV), lambda g: (g, 0, 0)),
        ],
        out_shape=[
            jax.ShapeDtypeStruct((batch * seq, W_B_V), BF16),
            jax.ShapeDtypeStruct((depth, batch, H_B, DK_B, DV_B), F32),
            jax.ShapeDtypeStruct((batch, CONV_W - 1, C_CONV), F32),
        ],
        scratch_shapes=[
            pltpu.VMEM((nb * 2 * SUBLANES, C_CONV), F32),
            pltpu.VMEM((n, C_CONV), F32),
        ],
        input_output_aliases={8: 1} if aliased else {},
        compiler_params=pltpu.CompilerParams(
            dimension_semantics=("parallel",), vmem_limit_bytes=VMEM_LIMIT),
        name="gdn_sample",
    )(*args)


def _pack_w_in(w_in):
    w = w_in.astype(BF16)
    o = 0
    parts = {}
    for name, width in (("qa", W_A_QK), ("ka", W_A_QK), ("va", W_A_V), ("ip", H_A), ("fp", H_A), ("oa", W_A_V),
                        ("qkvb", C_CONV), ("ab", H_B), ("betab", H_B), ("zb", W_B_V), ("ga", D_MODEL),
                        ("gb", D_MODEL)):
        parts[name] = w[:, :, o:o + width]
        o += width
    main = jnp.concatenate([parts[k] for k in ("qa", "ka", "va", "oa", "qkvb", "zb", "ga", "gb")], axis=-1)
    small = jnp.concatenate([parts[k] for k in ("ip", "fp", "ab", "betab")], axis=-1)
    small = jnp.pad(small, ((0, 0), (0, 0), (0, LANES - small.shape[-1])))
    return main, small


def _param_rows(b_igate, b_fgate, dt_bias, a_log):
    bias = jnp.concatenate([b_igate, b_fgate, dt_bias], axis=-1)
    bias = jnp.pad(bias, ((0, 0), (0, LANES - bias.shape[-1])))
    alog = jnp.pad(a_log, ((0, 0), (SM_A, LANES - SM_A - H_B)))
    rows = jnp.stack([bias, alog], axis=1)
    return jnp.pad(rows, ((0, 0), (0, SUBLANES - 2), (0, 0))).astype(F32)


def kernel(x_prompt, x_sample, p_prompt, p_sample, state_mlstm_C, state_mlstm_n, state_mlstm_m, state_gdn_S,
           state_gdn_conv, g_mix, w_in, b_igate, b_fgate, g_norm_a, conv_w, a_log, dt_bias, g_norm_b, w_pa, w_pb,
           w_out, g_ffn, w_up, w_down, g_ple, w_pg, w_pp, g_final):
    depth = w_in.shape[0]
    bp, tp, _ = x_prompt.shape
    bs, ts, _ = x_sample.shape
    w_main, w_small = _pack_w_in(w_in)
    prm = _param_rows(b_igate, b_fgate, dt_bias, a_log)
    w = {k: v.astype(BF16) for k, v in (("w_pa", w_pa), ("w_pb", w_pb), ("w_out", w_out), ("w_up", w_up),
                                          ("w_down", w_down), ("w_pg", w_pg), ("w_pp", w_pp))}
    w["g_ffn"] = g_ffn.reshape(depth, 1, D_MODEL)
    w["g_ple"] = g_ple.reshape(depth, 1, D_MODEL)
    w["g_final"] = g_final.reshape(1, D_MODEL)
    g_in = g_mix.reshape(depth, 1, D_MODEL)
    gna = g_norm_a.reshape(depth, 1, W_A_V)
    gnb = g_norm_b.reshape(depth, 1, DV_B)
    pp = p_prompt.reshape(depth, bp * tp, PLE_DIM)
    ps = p_sample.reshape(depth, bs * ts, PLE_DIM)
    xp = x_prompt.reshape(bp * tp, D_MODEL)
    xs = x_sample.reshape(bs * ts, D_MODEL)
    outs = {k: [] for k in ("pC", "pn", "pm", "pS", "pc", "sn", "sm", "sc")}
    c_stack = s_stack = None
    for i in range(depth):
        final = i == depth - 1

        pj, pj_s = _in_proj(xp, g_in, w_main, w_small, i)
        ha, c1, n1, m1 = _mlstm_prompt(pj, pj_s, prm, gna, i, bp, tp)
        hb, s1, v1 = _gdn_prompt(pj, pj_s, conv_w, prm, gnb, i, bp, tp)
        xp = _post(xp, ha, hb, pj, pp, w, i, final)
        outs["pC"].append(c1)
        outs["pn"].append(n1)
        outs["pm"].append(m1[:, :H_A, 0])
        outs["pS"].append(s1)
        outs["pc"].append(v1)

        qj, qj_s = _in_proj(xs, g_in, w_main, w_small, i)
        ha, c_stack, n2, m2 = _mlstm_sample(qj, qj_s, prm, gna, state_mlstm_C, state_mlstm_n, state_mlstm_m,
                                            c_stack, i, bs, ts)
        hb, s_stack, v2 = _gdn_sample(qj, qj_s, conv_w, prm, gnb, state_gdn_S, state_gdn_conv, s_stack,
                                      i, bs, ts)
        xs = _post(xs, ha, hb, qj, ps, w, i, final)
        outs["sn"].append(n2)
        outs["sm"].append(m2)
        outs["sc"].append(v2)

    st = {k: jnp.stack(v) for k, v in outs.items()}
    return (xp.reshape(bp, tp, D_MODEL), xs.reshape(bs, ts, D_MODEL),
            st["pC"], st["pn"], st["pm"], st["pS"], st["pc"],
            c_stack, st["sn"], st["sm"], s_stack, st["sc"])
```

```python
import functools
import math

import jax
import jax.numpy as jnp
from jax import lax
from jax.experimental import pallas as pl
from jax.experimental.pallas import tpu as pltpu

F32 = jnp.float32
BF16 = jnp.bfloat16

D_MODEL = 1024
H_A, DK_A, DV_A = 4, 128, 256
H_B, DK_B, DV_B = 8, 128, 128
W_A_QK, W_A_V = H_A * DK_A, H_A * DV_A
W_B_QK, W_B_V = H_B * DK_B, H_B * DV_B
CONV_W = 4
C_CONV = 2 * W_B_QK + W_B_V
D_FF = 4 * D_MODEL
PLE_DIM = 256
CHUNK = 64
EPS = 1e-6

LANES = 128
SUBLANES = 8

COL_QA, COL_KA, COL_VA, COL_OA = 0, 512, 1024, 2048
COL_QKVB, COL_ZB, COL_GA, COL_GB = 3072, 6144, 7168, 8192
N_MAIN = 9216
SM_I, SM_F, SM_A, SM_BETA = 0, 4, 8, 16

TILE_T = 256
GRID_T = 1024
VMEM_LIMIT = 56 * 1024 * 1024


def _bf(x):
    return x.astype(BF16)


def _f32(x):
    return x.astype(F32)


def _mm(a, b):
    return jnp.dot(a, b, preferred_element_type=F32)


def _mm_nt(a, b):
    return lax.dot_general(a, b, (((1,), (1,)), ((), ())), preferred_element_type=F32)


def _mm_tn(a, b):
    return lax.dot_general(a, b, (((0,), (0,)), ((), ())), preferred_element_type=F32)


def _split3(x):
    hi = _bf(x)
    r = x - _f32(hi)
    mid = _bf(r)
    lo = _bf(r - _f32(mid))
    return hi, mid, lo


def _sigmoid(x):
    return 1.0 / (1.0 + jnp.exp2(x * (-math.log2(math.e))))


def _softplus(x):
    return jnp.maximum(x, 0.0) + jnp.log1p(jnp.exp(-jnp.abs(x)))


def _rmsnorm(x, g):
    return x * lax.rsqrt(jnp.mean(x * x, axis=-1, keepdims=True) + EPS) * g


def _iota(shape, dim):
    return lax.broadcasted_iota(jnp.int32, shape, dim)


def _seg_masks(n, seg, reps=1):
    row = _iota((n, reps * n), 0)
    col = _iota((n, reps * n), 1) % n
    same = True if seg == n else (row // seg) == (col // seg)
    lower = jnp.logical_and(col <= row, same)
    strict = jnp.logical_and(col < row, same)
    upper = jnp.logical_and(col >= row, same)
    return lower, strict, upper, row == col


def _one_hot_bf(mask):
    return jnp.where(mask, 1.0, 0.0).astype(BF16)


def _rr(*gens):
    active = list(gens)
    while active:
        for g in list(active):
            try:
                next(g)
            except StopIteration:
                active.remove(g)
        yield


def _run(*gens):
    for _ in _rr(*gens):
        pass


def _in_kernel(x_ref, g_ref, wm_ref, ws_ref, om_ref, os_ref, h_ref):
    @pl.when(pl.program_id(1) == 0)
    def _():
        h_ref[...] = _bf(_rmsnorm(x_ref[...], g_ref[...]))
        os_ref[...] = _mm(h_ref[...], ws_ref[...])

    om_ref[...] = _bf(_mm(h_ref[...], wm_ref[...]))


def _in_proj(x, g, w_main, w_small, layer):
    m = x.shape[0]
    tm = min(m, 2048)
    tn = 1024
    return pl.pallas_call(
        _in_kernel,
        grid=(m // tm, N_MAIN // tn),
        in_specs=[
            pl.BlockSpec((tm, D_MODEL), lambda i, j: (i, 0)),
            pl.BlockSpec((None, 1, D_MODEL), lambda i, j: (layer, 0, 0)),
            pl.BlockSpec((None, D_MODEL, tn), lambda i, j: (layer, 0, j)),
            pl.BlockSpec((None, D_MODEL, LANES), lambda i, j: (layer, 0, 0)),
        ],
        out_specs=[
            pl.BlockSpec((tm, tn), lambda i, j: (i, j)),
            pl.BlockSpec((tm, LANES), lambda i, j: (i, 0)),
        ],
        out_shape=[jax.ShapeDtypeStruct((m, N_MAIN), BF16), jax.ShapeDtypeStruct((m, LANES), F32)],
        scratch_shapes=[pltpu.VMEM((tm, D_MODEL), BF16)],
        compiler_params=pltpu.CompilerParams(
            dimension_semantics=("parallel", "arbitrary"), vmem_limit_bytes=VMEM_LIMIT),
        name="in_proj",
    )(x, g, w_main, w_small)


def _post_kernel(x_ref, ha_ref, hb_ref, ga_ref, gb_ref, p_ref, wpa_ref, wpb_ref, wout_ref, gffn_ref,
                 wup_ref, wdown_ref, gple_ref, wpg_ref, wpp_ref, gfin_ref, o_ref, *, final):
    x = x_ref[...]
    a = _mm(ha_ref[...], wpa_ref[...])
    b = _mm(hb_ref[...], wpb_ref[...])
    mix = _sigmoid(_f32(ga_ref[...])) * a + _sigmoid(_f32(gb_ref[...])) * b
    x = x + _mm(_bf(mix), wout_ref[...])
    hn = _bf(_rmsnorm(x, gffn_ref[...]))
    acc = jnp.zeros_like(x)
    fc = 1024
    for c in range(D_FF // fc):
        u = _mm(hn, wup_ref[:, c * fc:(c + 1) * fc])
        r = jnp.maximum(u, 0.0)
        acc = acc + _mm(_bf(r * r), wdown_ref[c * fc:(c + 1) * fc, :])
    x = x + acc
    hp = _bf(_rmsnorm(x, gple_ref[...]))
    gate = _sigmoid(_mm(hp, wpg_ref[...]))
    x = x + gate * _mm(_bf(p_ref[...]), wpp_ref[...])
    if final:
        x = _rmsnorm(x, gfin_ref[...])
    o_ref[...] = x


def _post(x, ha, hb, proj_main, p, w, layer, final):
    m = x.shape[0]
    tm = min(m, 512)
    row = lambda i: (i, 0)
    wspec = lambda r, c: pl.BlockSpec((None, r, c), lambda i: (layer, 0, 0), pipeline_mode=pl.Buffered(1))
    return pl.pallas_call(
        functools.partial(_post_kernel, final=final),
        grid=(m // tm,),
        in_specs=[
            pl.BlockSpec((tm, D_MODEL), row),
            pl.BlockSpec((tm, W_A_V), row),
            pl.BlockSpec((tm, W_B_V), row),
            pl.BlockSpec((tm, D_MODEL), lambda i: (i, COL_GA // D_MODEL)),
            pl.BlockSpec((tm, D_MODEL), lambda i: (i, COL_GB // D_MODEL)),
            pl.BlockSpec((None, tm, PLE_DIM), lambda i: (layer, i, 0)),
            wspec(W_A_V, D_MODEL),
            wspec(W_B_V, D_MODEL),
            wspec(D_MODEL, D_MODEL),
            wspec(1, D_MODEL),
            wspec(D_MODEL, D_FF),
            wspec(D_FF, D_MODEL),
            wspec(1, D_MODEL),
            wspec(D_MODEL, D_MODEL),
            wspec(PLE_DIM, D_MODEL),
            pl.BlockSpec((1, D_MODEL), lambda i: (0, 0), pipeline_mode=pl.Buffered(1)),
        ],
        out_specs=pl.BlockSpec((tm, D_MODEL), row),
        out_shape=jax.ShapeDtypeStruct((m, D_MODEL), F32),
        compiler_params=pltpu.CompilerParams(
            dimension_semantics=("parallel",), vmem_limit_bytes=VMEM_LIMIT),
        name="post_final" if final else "post",
    )(x, ha, hb, proj_main, proj_main, p, w["w_pa"], w["w_pb"], w["w_out"], w["g_ffn"], w["w_up"],
      w["w_down"], w["g_ple"], w["w_pg"], w["w_pp"], w["g_final"])


def _gate_tables(small, prm, kind, lower_bf, upper_bf, eye_bf):
    n = small.shape[0]
    lane = _iota((n, LANES), 1)
    p = small + prm[0:1, :]
    if kind == "mlstm":
        lf = -_softplus(-p)
        pm = jnp.where(lane < SM_F, p, jnp.where(lane < SM_A, lf, 0.0))
    else:
        g = -jnp.exp(prm[1:2, :]) * _softplus(p)
        beta = _sigmoid(p)
        in_a = jnp.logical_and(lane >= SM_A, lane < SM_BETA)
        in_b = jnp.logical_and(lane >= SM_BETA, lane < SM_BETA + H_B)
        pm = jnp.where(in_a, g, jnp.where(in_b, beta, 0.0))
    parts = _split3(pm)
    cs = sum(_mm(lower_bf, q) for q in parts)
    pm_t = sum(_mm_tn(q, eye_bf) for q in parts) if eye_bf is not None else None
    cs_t = sum(_mm_tn(q, upper_bf) for q in parts)
    return pm, cs, pm_t, cs_t


def _mlstm_out(num, nq, m_row, gna, o):
    den = jnp.maximum(jnp.abs(nq), jnp.exp(-m_row))
    hh = num / den
    hr = hh * lax.rsqrt(jnp.mean(hh * hh, axis=1, keepdims=True) + EPS)
    return hr * gna * _sigmoid(o)


def _mlstm_tile_gen(q_ref, k_ref, v_ref, o_ref, sm_ref, gna_ref, ha_ref, prm, ch, nh, mh, n_chunks, rows_of):
    L = CHUNK
    hs = range(H_A)
    lower, _, upper, diag = _seg_masks(L, L)
    lower_bf, upper_bf, eye_bf = _one_hot_bf(lower), _one_hot_bf(upper), _one_hot_bf(diag)
    scale = DK_A ** -0.5
    for c in range(n_chunks):
        rows = rows_of(c)
        pm, cs, pm_t, cs_t = _gate_tables(sm_ref[rows, :], prm, "mlstm", lower_bf, upper_bf, eye_bf)
        qb = [q_ref[rows, h * DK_A:(h + 1) * DK_A] for h in hs]
        kb = [k_ref[rows, h * DK_A:(h + 1) * DK_A] for h in hs]
        vb = [v_ref[rows, h * DV_A:(h + 1) * DV_A] for h in hs]
        qk = [_mm_nt(qb[h], kb[h]) for h in hs]
        yield
        b_row, a_row, s_loc, kw_loc = [], [], [], []
        for h in hs:
            i_row = pm[:, SM_I + h:SM_I + h + 1]
            b_row.append(cs[:, SM_F + h:SM_F + h + 1])
            dlog = jnp.where(
                lower, b_row[h] - cs_t[SM_F + h:SM_F + h + 1, :] + pm_t[SM_I + h:SM_I + h + 1, :], -jnp.inf)
            a_row.append(jnp.max(dlog, axis=1, keepdims=True))
            s_loc.append(qk[h] * (scale * jnp.exp(dlog - a_row[h])))
            b_last = b_row[h][L - 1:L, :]
            kw_loc.append(_f32(kb[h]) * jnp.exp(i_row + b_last - b_row[h] - a_row[h][L - 1:L, :]))
        sv = [_mm(_bf(s_loc[h]), vb[h]) for h in hs]
        kv = [_mm_tn(_bf(kw_loc[h]), vb[h]) for h in hs]
        qc = [_mm(qb[h], _bf(ch[h])) for h in hs]
        yield
        for h in hs:
            m_row = jnp.maximum(b_row[h] + mh[h], a_row[h])
            w_inter = scale * jnp.exp(b_row[h] + mh[h] - m_row)
            f_intra = jnp.exp(a_row[h] - m_row)
            num = w_inter * qc[h] + f_intra * sv[h]
            nq = (w_inter * jnp.sum(_f32(qb[h]) * nh[h], axis=1, keepdims=True)
                  + f_intra * jnp.sum(s_loc[h], axis=1, keepdims=True))
            ha_ref[rows, h * DV_A:(h + 1) * DV_A] = _bf(_mlstm_out(
                num, nq, m_row, gna_ref[:, h * DV_A:(h + 1) * DV_A], _f32(o_ref[rows, h * DV_A:(h + 1) * DV_A])))
            m_new = m_row[L - 1:L, :]
            dec = jnp.exp(b_row[h][L - 1:L, :] + mh[h] - m_new)
            g_loc = jnp.exp(a_row[h][L - 1:L, :] - m_new)
            ch[h] = dec * ch[h] + g_loc * kv[h]
            nh[h] = dec * nh[h] + g_loc * jnp.sum(kw_loc[h], axis=0, keepdims=True)
            mh[h] = m_new
        yield


def _rep_rows(x, seg):
    g = x.shape[0]
    return jnp.concatenate([jnp.broadcast_to(x[j:j + 1, :], (seg, x.shape[1])) for j in range(g)], axis=0)


def _mlstm_sample_kernel(*refs, seg, aliased):
    (q_ref, k_ref, v_ref, o_ref, sm_ref, prm_ref, gna_ref, c0_ref, n0_ref, m0_ref) = refs[:10]
    ha_ref, c_out, n_out, m_out = refs[10 + aliased:]
    n = q_ref.shape[0]
    nb = n // seg
    hs, js = range(H_A), range(nb)
    sl = [slice(j * seg, (j + 1) * seg) for j in js]
    last = [slice((j + 1) * seg - 1, (j + 1) * seg) for j in js]
    lower, _, upper, diag = _seg_masks(n, seg)
    lower_bf, upper_bf, eye_bf = _one_hot_bf(lower), _one_hot_bf(upper), _one_hot_bf(diag)
    scale = DK_A ** -0.5
    pm, cs, pm_t, cs_t = _gate_tables(sm_ref[...], prm_ref[...], "mlstm", lower_bf, upper_bf, eye_bf)
    m_rep = _rep_rows(m0_ref[...], seg)
    qb = [q_ref[:, h * DK_A:(h + 1) * DK_A] for h in hs]
    kb = [k_ref[:, h * DK_A:(h + 1) * DK_A] for h in hs]
    vb = [v_ref[:, h * DV_A:(h + 1) * DV_A] for h in hs]
    i_row = [pm[:, SM_I + h:SM_I + h + 1] for h in hs]
    b_row = [cs[:, SM_F + h:SM_F + h + 1] for h in hs]
    qk = [_mm_nt(qb[h], kb[h]) for h in hs]
    qc = [jnp.concatenate([_mm(qb[h][sl[j], :], _bf(c0_ref[j, h])) for j in js], axis=0) for h in hs]
    m_row, w_inter, s = [], [], []
    for h in hs:
        dlog = jnp.where(lower, b_row[h] - cs_t[SM_F + h:SM_F + h + 1, :] + pm_t[SM_I + h:SM_I + h + 1, :],
                         -jnp.inf)
        inter = b_row[h] + m_rep[:, h:h + 1]
        m_row.append(jnp.maximum(inter, jnp.max(dlog, axis=1, keepdims=True)))
        w_inter.append(scale * jnp.exp(inter - m_row[h]))
        s.append(qk[h] * (scale * jnp.exp(dlog - m_row[h])))
    sv = [_mm(_bf(s[h]), vb[h]) for h in hs]
    for h in hs:
        n_rep = _rep_rows(n0_ref[:, h, :], seg)
        num = w_inter[h] * qc[h] + sv[h]
        nq = (w_inter[h] * jnp.sum(_f32(qb[h]) * n_rep, axis=1, keepdims=True)
              + jnp.sum(s[h], axis=1, keepdims=True))
        ha_ref[:, h * DV_A:(h + 1) * DV_A] = _bf(_mlstm_out(
            num, nq, m_row[h], gna_ref[:, h * DV_A:(h + 1) * DV_A], _f32(o_ref[:, h * DV_A:(h + 1) * DV_A])))
    for h in hs:
        kf = _f32(kb[h])
        for j in js:
            m_new = m_row[h][last[j], :]
            b_last = b_row[h][last[j], :]
            dec = jnp.exp(b_last + m0_ref[j:j + 1, h:h + 1] - m_new)
            kw = kf[sl[j], :] * jnp.exp(i_row[h][sl[j], :] + b_last - b_row[h][sl[j], :] - m_new)
            c_out[j, h] = dec * c0_ref[j, h] + _mm_tn(_bf(kw), vb[h][sl[j], :])
            n_out[j, h:h + 1, :] = dec * n0_ref[j, h:h + 1, :] + jnp.sum(kw, axis=0, keepdims=True)
            m_out[j:j + 1, h:h + 1] = m_new


def _mlstm_sample(proj_main, proj_small, prm, gna, c0, n0, m0, c_stack, layer, batch, seq):
    depth = c0.shape[0]
    nb = CHUNK // seq
    n = nb * seq
    rb = lambda col: (lambda g: (g, col))
    par = lambda g: (layer, 0, 0)
    aliased = c_stack is not None
    in_specs = [
        pl.BlockSpec((n, W_A_QK), rb(COL_QA // W_A_QK)),
        pl.BlockSpec((n, W_A_QK), rb(COL_KA // W_A_QK)),
        pl.BlockSpec((n, W_A_V), rb(COL_VA // W_A_V)),
        pl.BlockSpec((n, W_A_V), rb(COL_OA // W_A_V)),
        pl.BlockSpec((n, LANES), rb(0)),
        pl.BlockSpec((None, SUBLANES, LANES), par),
        pl.BlockSpec((None, 1, W_A_V), par),
        pl.BlockSpec((None, nb, H_A, DK_A, DV_A), lambda g: (layer, g, 0, 0, 0)),
        pl.BlockSpec((None, nb, H_A, DK_A), lambda g: (layer, g, 0, 0)),
        pl.BlockSpec((None, nb, H_A), lambda g: (layer, g, 0)),
    ]
    args = [proj_main, proj_main, proj_main, proj_main, proj_small, prm, gna, c0, n0, m0]
    if aliased:
        in_specs.append(pl.BlockSpec(memory_space=pl.ANY))
        args.append(c_stack)
    return pl.pallas_call(
        functools.partial(_mlstm_sample_kernel, seg=seq, aliased=int(aliased)),
        grid=(batch // nb,),
        in_specs=in_specs,
        out_specs=[
            pl.BlockSpec((n, W_A_V), rb(0)),
            pl.BlockSpec((None, nb, H_A, DK_A, DV_A), lambda g: (layer, g, 0, 0, 0)),
            pl.BlockSpec((nb, H_A, DK_A), lambda g: (g, 0, 0)),
            pl.BlockSpec((nb, H_A), lambda g: (g, 0)),
        ],
        out_shape=[
            jax.ShapeDtypeStruct((batch * seq, W_A_V), BF16),
            jax.ShapeDtypeStruct((depth, batch, H_A, DK_A, DV_A), F32),
            jax.ShapeDtypeStruct((batch, H_A, DK_A), F32),
            jax.ShapeDtypeStruct((batch, H_A), F32),
        ],
        input_output_aliases={10: 1} if aliased else {},
        compiler_params=pltpu.CompilerParams(
            dimension_semantics=("parallel",), vmem_limit_bytes=VMEM_LIMIT),
        name="mlstm_sample",
    )(*args)


def _conv_silu(xp_ref, w_ref, cv_ref, src0, dst0, n_rows):
    cb = 512
    top = src0 + CONV_W - 1 - SUBLANES
    for c0 in range(0, C_CONV, cb):
        cols = slice(c0, c0 + cb)
        xx = xp_ref[top:top + SUBLANES + n_rows, cols]
        acc = w_ref[CONV_W - 1:CONV_W, cols] * xx[SUBLANES:, :]
        for j in range(1, CONV_W):
            acc = acc + w_ref[CONV_W - 1 - j:CONV_W - j, cols] * pltpu.roll(xx, j, axis=0)[SUBLANES:, :]
        cv_ref[dst0:dst0 + n_rows, cols] = acc * _sigmoid(acc)


CONV_HIST = 16


def _shift_select(n_rows):
    shape = ((CONV_W - 1) * n_rows, CONV_HIST + n_rows)
    r, col = _iota(shape, 0), _iota(shape, 1)
    return _one_hot_bf(col == CONV_HIST + r % n_rows - (r // n_rows + 1))


def _conv_silu_mxu(xb_ref, w_ref, cv_ref, top, dst0, n_rows, shift_bf):
    cb = 512
    for c0 in range(0, C_CONV, cb):
        cols = slice(c0, c0 + cb)
        xx = xb_ref[top:top + CONV_HIST + n_rows, cols]
        sh = _mm(shift_bf, xx)
        acc = w_ref[CONV_W - 1:CONV_W, cols] * _f32(xx[CONV_HIST:, :])
        for j in range(1, CONV_W):
            acc = acc + w_ref[CONV_W - 1 - j:CONV_W - j, cols] * sh[(j - 1) * n_rows:j * n_rows, :]
        cv_ref[dst0:dst0 + n_rows, cols] = acc * _sigmoid(acc)


def _bd_rhs(x_f, first):
    return jnp.concatenate([_bf(jnp.where(first, x_f, 0.0)), _bf(jnp.where(first, 0.0, x_f))], axis=0)


def _pair_split(x2, first, want_lhs=True, want_rhs=True):
    hi = _bf(x2)
    hi_f = _f32(hi)
    lo_f = x2 - hi_f
    lhs3 = jnp.concatenate([hi, hi, _bf(lo_f)], axis=1) if want_lhs else None
    rhs3 = None
    if want_rhs:
        bd_hi = _bd_rhs(hi_f, first)
        rhs3 = jnp.concatenate([bd_hi, _bd_rhs(lo_f, first), bd_hi], axis=0)
    return lhs3, rhs3


def _bd2(a, b):
    z = jnp.zeros_like(a)
    return jnp.concatenate([jnp.concatenate([a, z], axis=1), jnp.concatenate([z, b], axis=1)], axis=0)


def _gdn_intra_gen(load_cv, tabs, masks, n_sq, res):
    lower2, strict2, diag2, first = masks
    items = [(b, p) for b in range(len(tabs)) for p in range(H_B // 2)]
    gam_row, decay2, eg, qn, kn, kbv, rhs, kq = {}, {}, {}, {}, {}, {}, {}, {}
    for b, p in items:
        pm, cs, _, cs_t2 = tabs[b]
        hh = (2 * p, 2 * p + 1)
        gam_row[b, p] = [cs[:, SM_A + h:SM_A + h + 1] for h in hh]
        beta_row = [pm[:, SM_BETA + h:SM_BETA + h + 1] for h in hh]
        gam_lane2 = jnp.where(first[0:1, :], cs_t2[SM_A + hh[0]:SM_A + hh[0] + 1, :],
                              cs_t2[SM_A + hh[1]:SM_A + hh[1] + 1, :])
        gam_row2 = jnp.where(first, gam_row[b, p][0], gam_row[b, p][1])
        decay2[b, p] = jnp.exp(jnp.where(lower2, gam_row2 - gam_lane2, -jnp.inf))
        eg[b, p] = [jnp.exp(g) for g in gam_row[b, p]]
        cq2, ck2, vv2 = load_cv(b, p, 0), load_cv(b, p, 1), load_cv(b, p, 2)
        qn[b, p], kn[b, p], kbv[b, p], rhs[b, p] = [], [], [], []
        for e in range(2):
            cq, ck, vv = (x[:, e * DK_B:(e + 1) * DK_B] for x in (cq2, ck2, vv2))
            qn[b, p].append(cq * (lax.rsqrt(jnp.sum(cq * cq, axis=1, keepdims=True) + EPS) * (DK_B ** -0.5)))
            kn[b, p].append(ck * lax.rsqrt(jnp.sum(ck * ck, axis=1, keepdims=True) + EPS))
            kbv[b, p].append(kn[b, p][e] * beta_row[e])
            rhs[b, p].append(jnp.concatenate([vv * beta_row[e], kbv[b, p][e] * eg[b, p][e]], axis=1))
        lhs = jnp.concatenate([_bf(jnp.concatenate(kbv[b, p], axis=1)), _bf(jnp.concatenate(qn[b, p], axis=1))],
                              axis=0)
        kq[b, p] = _mm_nt(lhs, _bd2(_bf(kn[b, p][0]), _bf(kn[b, p][1])))
    yield
    n = lower2.shape[0]
    eye2 = jnp.where(diag2, 1.0, 0.0)
    pw, tinv, l_pw, r_pw = {}, {}, {}, {}
    for it in items:
        pw[it] = jnp.where(strict2, -(kq[it][0:n, :] * decay2[it]), 0.0)
        tinv[it] = eye2 + pw[it]
        l_pw[it], r_pw[it] = _pair_split(pw[it], first)
    for step in range(n_sq):
        for it in items:
            pw[it] = _mm(l_pw[it], r_pw[it])
        yield
        for it in items:
            l_pw[it], r_pw[it] = _pair_split(pw[it], first, want_lhs=step + 1 < n_sq)
            tinv[it] = tinv[it] + _mm(_pair_split(tinv[it], first, want_rhs=False)[0], r_pw[it])
        yield
    sol = {}
    for it in items:
        parts = []
        for e in range(2):
            hi = _bf(rhs[it][e])
            parts.append((hi, _bf(rhs[it][e] - _f32(hi))))
        bd_hi = _bd2(parts[0][0], parts[1][0])
        rhs3 = jnp.concatenate([bd_hi, _bd2(parts[0][1], parts[1][1]), bd_hi], axis=0)
        sol[it] = _mm(_pair_split(tinv[it], first, want_rhs=False)[0], rhs3)
    yield
    for it in items:
        w_v = [sol[it][:, e * 2 * DV_B:e * 2 * DV_B + DV_B] for e in range(2)]
        w_k = [_bf(sol[it][:, e * 2 * DV_B + DV_B:(e + 1) * 2 * DV_B]) for e in range(2)]
        q_dec = [_bf(qn[it][e] * eg[it][e]) for e in range(2)]
        res[it] = (q_dec, kn[it], w_v, w_k, _bf(kq[it][n:, :] * decay2[it]), gam_row[it])


def _gdn_scan_pair(res_it, s_pair, g_l):
    q_dec, kn, w_v, w_k, qk2, gam_row = res_it
    n = w_v[0].shape[0]
    r = [_mm(jnp.concatenate([w_k[e], q_dec[e]], axis=0), _bf(s_pair[e])) for e in range(2)]
    yield
    u = [_bf(w_v[e] - r[e][0:n, :]) for e in range(2)]
    o2 = jnp.concatenate([r[0][n:, :], r[1][n:, :]], axis=1) + _mm(qk2, _bd2(u[0], u[1]))
    for e in range(2):
        k_dec = _bf(kn[e] * jnp.exp(g_l[e] - gam_row[e]))
        s_pair[e] = s_pair[e] * jnp.exp(g_l[e]) + _mm_tn(k_dec, u[e])
    yield
    return o2


def _gdn_out(o, gnb, z):
    return o * lax.rsqrt(jnp.mean(o * o, axis=1, keepdims=True) + EPS) * gnb * (z * _sigmoid(z))


def _gdn_tile_gen(z_ref, sm_ref, gnb_ref, hb_ref, cv_s, prm, state, n_chunks, io_rows):
    L = CHUNK
    ps = range(H_B // 2)
    rows = [slice(c * L, (c + 1) * L) for c in range(n_chunks)]
    io = [io_rows(c) for c in range(n_chunks)]
    lower, _, _, _ = _seg_masks(L, L)
    lower2, strict2, upper2, diag2 = _seg_masks(L, L, reps=2)
    first = _iota((L, 2 * L), 1) < L
    masks = (lower2, strict2, diag2, first)
    lower_bf, upper2_bf = _one_hot_bf(lower), _one_hot_bf(upper2)
    n_sq = int(math.log2(L)) - 1
    res, outs = {}, {}

    def intra(chunks):
        tabs = [_gate_tables(sm_ref[io[c], :], prm, "gdn", lower_bf, upper2_bf, None) for c in chunks]
        load = lambda b, p, part: _f32(cv_s[rows[chunks[b]],
                                            part * W_B_QK + 2 * p * DK_B:part * W_B_QK + 2 * (p + 1) * DK_B])
        sub = {}
        yield from _gdn_intra_gen(load, tabs, masks, n_sq, sub)
        for (b, p), v in sub.items():
            res[chunks[b], p] = v

    def scan(chunks):
        for c in chunks:
            gens = []
            for p in ps:
                g_l = [g[L - 1:L, :] for g in res[c, p][5]]
                gens.append(_gdn_scan_pair(res[c, p], state[p], g_l))
            for stage in range(2):
                for g in gens:
                    next(g)
                yield
            for p, g in zip(ps, gens):
                try:
                    next(g)
                except StopIteration as done:
                    outs[c, p] = done.value

    def emit(chunks):
        for c in chunks:
            for p in ps:
                for e in range(2):
                    h = 2 * p + e
                    hb_ref[io[c], h * DV_B:(h + 1) * DV_B] = _bf(_gdn_out(
                        outs[c, p][:, e * DV_B:(e + 1) * DV_B], gnb_ref[...],
                        _f32(z_ref[io[c], h * DV_B:(h + 1) * DV_B])))

    half = n_chunks // 2
    first_half, second_half = list(range(half)), list(range(half, n_chunks))
    yield from intra(first_half)
    yield from _rr(intra(second_half), scan(first_half))
    emit(first_half)
    yield from scan(second_half)
    emit(second_half)


def _mlstm_prompt_kernel(q_ref, k_ref, v_ref, o_ref, sm_ref, prm_ref, gna_ref,
                         ha_ref, c_out, n_out, m_out, c_s, n_s, m_s, *, n_chunks):
    t = pl.program_id(1)
    hs = range(H_A)

    @pl.when(t == 0)
    def _():
        c_s[...] = jnp.zeros_like(c_s)
        n_s[...] = jnp.zeros_like(n_s)
        m_s[...] = jnp.zeros_like(m_s)

    prm = prm_ref[...]

    def chunk(c, carry):
        ch = [c_s[h] for h in hs]
        nh = [n_s[h:h + 1, :] for h in hs]
        mh = [m_s[h:h + 1, 0:1] for h in hs]
        rows = pl.ds(pl.multiple_of(c * CHUNK, CHUNK), CHUNK)
        _run(_mlstm_tile_gen(q_ref, k_ref, v_ref, o_ref, sm_ref, gna_ref, ha_ref, prm, ch, nh, mh, 1,
                             lambda _: rows))
        for h in hs:
            c_s[h] = ch[h]
            n_s[h:h + 1, :] = nh[h]
            m_s[h:h + 1, :] = jnp.broadcast_to(mh[h], (1, LANES))
        return carry

    lax.fori_loop(0, n_chunks, chunk, 0)

    @pl.when(t == pl.num_programs(1) - 1)
    def _():
        c_out[0] = c_s[...]
        n_out[0] = n_s[0:H_A, :]
        m_out[0] = m_s[...]


def _mlstm_prompt(proj_main, proj_small, prm, gna, layer, batch, seq):
    tt = min(seq, GRID_T)
    nt = seq // tt
    rb = lambda col: (lambda b, t: (b * nt + t, col))
    par = lambda b, t: (layer, 0, 0)
    return pl.pallas_call(
        functools.partial(_mlstm_prompt_kernel, n_chunks=tt // CHUNK),
        grid=(batch, nt),
        in_specs=[
            pl.BlockSpec((tt, W_A_QK), rb(COL_QA // W_A_QK)),
            pl.BlockSpec((tt, W_A_QK), rb(COL_KA // W_A_QK)),
            pl.BlockSpec((tt, W_A_V), rb(COL_VA // W_A_V)),
            pl.BlockSpec((tt, W_A_V), rb(COL_OA // W_A_V)),
            pl.BlockSpec((tt, LANES), rb(0)),
            pl.BlockSpec((None, SUBLANES, LANES), par),
            pl.BlockSpec((None, 1, W_A_V), par),
        ],
        out_specs=[
            pl.BlockSpec((tt, W_A_V), rb(0)),
            pl.BlockSpec((1, H_A, DK_A, DV_A), lambda b, t: (b, 0, 0, 0)),
            pl.BlockSpec((1, H_A, DK_A), lambda b, t: (b, 0, 0)),
            pl.BlockSpec((1, SUBLANES, LANES), lambda b, t: (b, 0, 0)),
        ],
        out_shape=[
            jax.ShapeDtypeStruct((batch * seq, W_A_V), BF16),
            jax.ShapeDtypeStruct((batch, H_A, DK_A, DV_A), F32),
            jax.ShapeDtypeStruct((batch, H_A, DK_A), F32),
            jax.ShapeDtypeStruct((batch, SUBLANES, LANES), F32),
        ],
        scratch_shapes=[
            pltpu.VMEM((H_A, DK_A, DV_A), F32),
            pltpu.VMEM((SUBLANES, LANES), F32),
            pltpu.VMEM((SUBLANES, LANES), F32),
        ],
        compiler_params=pltpu.CompilerParams(
            dimension_semantics=("parallel", "arbitrary"), vmem_limit_bytes=VMEM_LIMIT),
        name="mlstm_prompt",
    )(proj_main, proj_main, proj_main, proj_main, proj_small, prm, gna)


def _gdn_prompt_kernel(x_ref, z_ref, sm_ref, cw_ref, prm_ref, gnb_ref,
                       hb_ref, s_out, conv_out, s_s, xp_s, cv_s, *, n_tiles, n_chunks):
    t = pl.program_id(1)
    L = CHUNK
    tt = n_chunks * L
    ps = range(H_B // 2)

    @pl.when(t == 0)
    def _():
        s_s[...] = jnp.zeros_like(s_s)
        xp_s[0:CONV_HIST, :] = jnp.zeros((CONV_HIST, C_CONV), BF16)

    prm = prm_ref[...]
    shift_bf = _shift_select(L)

    def tile(ti, carry):
        base = pl.multiple_of(ti * tt, tt)
        xp_s[CONV_HIST:CONV_HIST + tt, :] = x_ref[pl.ds(base, tt), :]
        for c in range(n_chunks):
            _conv_silu_mxu(xp_s, cw_ref, cv_s, c * L, c * L, L, shift_bf)
        state = [[s_s[2 * p], s_s[2 * p + 1]] for p in ps]
        _run(_gdn_tile_gen(z_ref, sm_ref, gnb_ref, hb_ref, cv_s, prm, state, n_chunks,
                           lambda c: pl.ds(pl.multiple_of(base + c * L, L), L)))
        for p in ps:
            s_s[2 * p] = state[p][0]
            s_s[2 * p + 1] = state[p][1]
        xp_s[0:CONV_HIST, :] = xp_s[tt:tt + CONV_HIST, :]
        return carry

    lax.fori_loop(0, n_tiles, tile, 0)

    @pl.when(t == pl.num_programs(1) - 1)
    def _():
        s_out[0] = s_s[...]
        conv_out[0] = _f32(xp_s[0:CONV_HIST, :])[CONV_HIST - (CONV_W - 1):, :]


def _gdn_prompt(proj_main, proj_small, conv_w, prm, gnb, layer, batch, seq):
    tt = TILE_T
    blk = min(seq, GRID_T)
    nt = seq // blk
    rb = lambda col: (lambda b, t: (b * nt + t, col))
    par = lambda b, t: (layer, 0, 0)
    return pl.pallas_call(
        functools.partial(_gdn_prompt_kernel, n_tiles=blk // tt, n_chunks=tt // CHUNK),
        grid=(batch, nt),
        in_specs=[
            pl.BlockSpec((blk, C_CONV), rb(COL_QKVB // C_CONV)),
            pl.BlockSpec((blk, W_B_V), rb(COL_ZB // W_B_V)),
            pl.BlockSpec((blk, LANES), rb(0)),
            pl.BlockSpec((None, CONV_W, C_CONV), par),
            pl.BlockSpec((None, SUBLANES, LANES), par),
            pl.BlockSpec((None, 1, DV_B), par),
        ],
        out_specs=[
            pl.BlockSpec((blk, W_B_V), rb(0)),
            pl.BlockSpec((1, H_B, DK_B, DV_B), lambda b, t: (b, 0, 0, 0)),
            pl.BlockSpec((1, CONV_W - 1, C_CONV), lambda b, t: (b, 0, 0)),
        ],
        out_shape=[
            jax.ShapeDtypeStruct((batch * seq, W_B_V), BF16),
            jax.ShapeDtypeStruct((batch, H_B, DK_B, DV_B), F32),
            jax.ShapeDtypeStruct((batch, CONV_W - 1, C_CONV), F32),
        ],
        scratch_shapes=[
            pltpu.VMEM((H_B, DK_B, DV_B), F32),
            pltpu.VMEM((tt + CONV_HIST, C_CONV), BF16),
            pltpu.VMEM((tt, C_CONV), F32),
        ],
        compiler_params=pltpu.CompilerParams(
            dimension_semantics=("parallel", "arbitrary"), vmem_limit_bytes=VMEM_LIMIT),
        name="gdn_prompt",
    )(proj_main, proj_main, proj_small, conv_w, prm, gnb)


def _gdn_sample_kernel(*refs, seg, aliased):
    x_ref, z_ref, sm_ref, cw_ref, prm_ref, gnb_ref, s0_ref, cb_ref = refs[:8]
    hb_ref, s_out, conv_out, xp_s, cv_s = refs[8 + aliased:]
    n = x_ref.shape[0]
    nb = n // seg
    ps, js = range(H_B // 2), range(nb)
    sl = [slice(j * seg, (j + 1) * seg) for j in js]
    stride = 2 * SUBLANES
    for j in js:
        base = j * stride + SUBLANES
        xp_s[base - SUBLANES:base, :] = jnp.zeros((SUBLANES, C_CONV), F32)
        xp_s[base - (CONV_W - 1):base, :] = cb_ref[j]
        xp_s[base:base + seg, :] = _f32(x_ref[sl[j], :])
        _conv_silu(xp_s, cw_ref, cv_s, base - (CONV_W - 1), j * seg, seg)
        conv_out[j] = xp_s[base + seg - (CONV_W - 1):base + seg, :]

    lower, _, _, _ = _seg_masks(n, seg)
    lower2, strict2, upper2, diag2 = _seg_masks(n, seg, reps=2)
    first = _iota((n, 2 * n), 1) < n
    masks = (lower2, strict2, diag2, first)
    n_sq = int(math.log2(seg)) - 1
    tabs = [_gate_tables(sm_ref[...], prm_ref[...], "gdn", _one_hot_bf(lower), _one_hot_bf(upper2), None)]
    load = lambda b, p, part: cv_s[:, part * W_B_QK + 2 * p * DK_B:part * W_B_QK + 2 * (p + 1) * DK_B]
    res = {}
    _run(_gdn_intra_gen(load, tabs, masks, n_sq, res))
    r, u = {}, {}
    for p in ps:
        q_dec, kn, w_v, w_k, qk2, gam_row = res[0, p]
        for e in range(2):
            h = 2 * p + e
            lhs = [jnp.concatenate([w_k[e][sl[j], :], q_dec[e][sl[j], :]], axis=0) for j in js]
            r[h] = [_mm(lhs[j], _bf(s0_ref[j, h])) for j in js]
    for p in ps:
        q_dec, kn, w_v, w_k, qk2, gam_row = res[0, p]
        for e in range(2):
            h = 2 * p + e
            u[h] = _bf(w_v[e] - jnp.concatenate([r[h][j][0:seg, :] for j in js], axis=0))
        oq = [jnp.concatenate([r[2 * p + e][j][seg:, :] for j in js], axis=0) for e in range(2)]
        o2 = jnp.concatenate(oq, axis=1) + _mm(qk2, _bd2(u[2 * p], u[2 * p + 1]))
        for e in range(2):
            h = 2 * p + e
            hb_ref[:, h * DV_B:(h + 1) * DV_B] = _bf(_gdn_out(
                o2[:, e * DV_B:(e + 1) * DV_B], gnb_ref[...], _f32(z_ref[:, h * DV_B:(h + 1) * DV_B])))
    for p in ps:
        q_dec, kn, w_v, w_k, qk2, gam_row = res[0, p]
        for e in range(2):
            h = 2 * p + e
            for j in js:
                g_l = gam_row[e][(j + 1) * seg - 1:(j + 1) * seg, :]
                k_dec = kn[e][sl[j], :] * jnp.exp(g_l - gam_row[e][sl[j], :])
                s_out[j, h] = s0_ref[j, h] * jnp.exp(g_l) + _mm_tn(_bf(k_dec), u[h][sl[j], :])


def _gdn_sample(proj_main, proj_small, conv_w, prm, gnb, s0, cb, s_stack, layer, batch, seq):
    depth = s0.shape[0]
    nb = CHUNK // seq
    n = nb * seq
    rb = lambda col: (lambda g: (g, col))
    par = lambda g: (layer, 0, 0)
    aliased = s_stack is not None
    in_specs = [
        pl.BlockSpec((n, C_CONV), rb(COL_QKVB // C_CONV)),
        pl.BlockSpec((n, W_B_V), rb(COL_ZB // W_B_V)),
        pl.BlockSpec((n, LANES), rb(0)),
        pl.BlockSpec((None, CONV_W, C_CONV), par),
        pl.BlockSpec((None, SUBLANES, LANES), par),
        pl.BlockSpec((None, 1, DV_B), par),
        pl.BlockSpec((None, nb, H_B, DK_B, DV_B), lambda g: (layer, g, 0, 0, 0)),
        pl.BlockSpec((None, nb, CONV_W - 1, C_CONV), lambda g: (layer, g, 0, 0)),
    ]
    args = [proj_main, proj_main, proj_small, conv_w, prm, gnb, s0, cb]
    if aliased:
        in_specs.append(pl.BlockSpec(memory_space=pl.ANY))
        args.append(s_stack)
    return pl.pallas_call(
        functools.partial(_gdn_sample_kernel, seg=seq, aliased=int(aliased)),
        grid=(batch // nb,),
        in_specs=in_specs,
        out_specs=[
            pl.BlockSpec((n, W_B_V), rb(0)),
            pl.BlockSpec((None, nb, H_B, DK_B, DV_B), lambda g: (layer, g, 0, 0, 0)),
            pl.BlockSpec((nb, CONV_W - 1, C_CONV), lambda g: (g, 0, 0)),
        ],
        out_shape=[
            jax.ShapeDtypeStruct((batch * seq, W_B_V), BF16),
            jax.ShapeDtypeStruct((depth, batch, H_B, DK_B, DV_B), F32),
            jax.ShapeDtypeStruct((batch, CONV_W - 1, C_CONV), F32),
        ],
        scratch_shapes=[
            pltpu.VMEM((nb * 2 * SUBLANES, C_CONV), F32),
            pltpu.VMEM((n, C_CONV), F32),
        ],
        input_output_aliases={8: 1} if aliased else {},
        compiler_params=pltpu.CompilerParams(
            dimension_semantics=("parallel",), vmem_limit_bytes=VMEM_LIMIT),
        name="gdn_sample",
    )(*args)


def _pack_w_in(w_in):
    w = w_in.astype(BF16)
    o = 0
    parts = {}
    for name, width in (("qa", W_A_QK), ("ka", W_A_QK), ("va", W_A_V), ("ip", H_A), ("fp", H_A), ("oa", W_A_V),
                        ("qkvb", C_CONV), ("ab", H_B), ("betab", H_B), ("zb", W_B_V), ("ga", D_MODEL),
                        ("gb", D_MODEL)):
        parts[name] = w[:, :, o:o + width]
        o += width
    main = jnp.concatenate([parts[k] for k in ("qa", "ka", "va", "oa", "qkvb", "zb", "ga", "gb")], axis=-1)
    small = jnp.concatenate([parts[k] for k in ("ip", "fp", "ab", "betab")], axis=-1)
    small = jnp.pad(small, ((0, 0), (0, 0), (0, LANES - small.shape[-1])))
    return main, small


def _param_rows(b_igate, b_fgate, dt_bias, a_log):
    bias = jnp.concatenate([b_igate, b_fgate, dt_bias], axis=-1)
    bias = jnp.pad(bias, ((0, 0), (0, LANES - bias.shape[-1])))
    alog = jnp.pad(a_log, ((0, 0), (SM_A, LANES - SM_A - H_B)))
    rows = jnp.stack([bias, alog], axis=1)
    return jnp.pad(rows, ((0, 0), (0, SUBLANES - 2), (0, 0))).astype(F32)


def kernel(x_prompt, x_sample, p_prompt, p_sample, state_mlstm_C, state_mlstm_n, state_mlstm_m, state_gdn_S,
           state_gdn_conv, g_mix, w_in, b_igate, b_fgate, g_norm_a, conv_w, a_log, dt_bias, g_norm_b, w_pa, w_pb,
           w_out, g_ffn, w_up, w_down, g_ple, w_pg, w_pp, g_final):
    depth = w_in.shape[0]
    bp, tp, _ = x_prompt.shape
    bs, ts, _ = x_sample.shape
    w_main, w_small = _pack_w_in(w_in)
    prm = _param_rows(b_igate, b_fgate, dt_bias, a_log)
    w = {k: v.astype(BF16) for k, v in (("w_pa", w_pa), ("w_pb", w_pb), ("w_out", w_out), ("w_up", w_up),
                                          ("w_down", w_down), ("w_pg", w_pg), ("w_pp", w_pp))}
    w["g_ffn"] = g_ffn.reshape(depth, 1, D_MODEL)
    w["g_ple"] = g_ple.reshape(depth, 1, D_MODEL)
    w["g_final"] = g_final.reshape(1, D_MODEL)
    g_in = g_mix.reshape(depth, 1, D_MODEL)
    gna = g_norm_a.reshape(depth, 1, W_A_V)
    gnb = g_norm_b.reshape(depth, 1, DV_B)
    pp = p_prompt.reshape(depth, bp * tp, PLE_DIM)
    ps = p_sample.reshape(depth, bs * ts, PLE_DIM)
    xp = x_prompt.reshape(bp * tp, D_MODEL)
    xs = x_sample.reshape(bs * ts, D_MODEL)
    outs = {k: [] for k in ("pC", "pn", "pm", "pS", "pc", "sn", "sm", "sc")}
    c_stack = s_stack = None
    for i in range(depth):
        final = i == depth - 1

        pj, pj_s = _in_proj(xp, g_in, w_main, w_small, i)
        ha, c1, n1, m1 = _mlstm_prompt(pj, pj_s, prm, gna, i, bp, tp)
        hb, s1, v1 = _gdn_prompt(pj, pj_s, conv_w, prm, gnb, i, bp, tp)
        xp = _post(xp, ha, hb, pj, pp, w, i, final)
        outs["pC"].append(c1)
        outs["pn"].append(n1)
        outs["pm"].append(m1[:, :H_A, 0])
        outs["pS"].append(s1)
        outs["pc"].append(v1)

        qj, qj_s = _in_proj(xs, g_in, w_main, w_small, i)
        ha, c_stack, n2, m2 = _mlstm_sample(qj, qj_s, prm, gna, state_mlstm_C, state_mlstm_n, state_mlstm_m,
                                            c_stack, i, bs, ts)
        hb, s_stack, v2 = _gdn_sample(qj, qj_s, conv_w, prm, gnb, state_gdn_S, state_gdn_conv, s_stack,
                                      i, bs, ts)
        xs = _post(xs, ha, hb, qj, ps, w, i, final)
        outs["sn"].append(n2)
        outs["sm"].append(m2)
        outs["sc"].append(v2)

    st = {k: jnp.stack(v) for k, v in outs.items()}
    return (xp.reshape(bp, tp, D_MODEL), xs.reshape(bs, ts, D_MODEL),
            st["pC"], st["pn"], st["pm"], st["pS"], st["pc"],
            c_stack, st["sn"], st["sm"], s_stack, st["sc"])
```

```python
import functools
import math

import jax
import jax.numpy as jnp
from jax import lax
from jax.experimental import pallas as pl
from jax.experimental.pallas import tpu as pltpu

F32 = jnp.float32
BF16 = jnp.bfloat16

D_MODEL = 1024
H_A, DK_A, DV_A = 4, 128, 256
H_B, DK_B, DV_B = 8, 128, 128
W_A_QK, W_A_V = H_A * DK_A, H_A * DV_A
W_B_QK, W_B_V = H_B * DK_B, H_B * DV_B
CONV_W = 4
C_CONV = 2 * W_B_QK + W_B_V
D_FF = 4 * D_MODEL
PLE_DIM = 256
CHUNK = 64
EPS = 1e-6

LANES = 128
SUBLANES = 8

COL_QA, COL_KA, COL_VA, COL_OA = 0, 512, 1024, 2048
COL_QKVB, COL_ZB, COL_GA, COL_GB = 3072, 6144, 7168, 8192
N_MAIN = 9216
SM_I, SM_F, SM_A, SM_BETA = 0, 4, 8, 16

TILE_T = 256
GRID_T = 1024
VMEM_LIMIT = 56 * 1024 * 1024


def _bf(x):
    return x.astype(BF16)


def _f32(x):
    return x.astype(F32)


def _mm(a, b):
    return jnp.dot(a, b, preferred_element_type=F32)


def _mm_nt(a, b):
    return lax.dot_general(a, b, (((1,), (1,)), ((), ())), preferred_element_type=F32)


def _mm_tn(a, b):
    return lax.dot_general(a, b, (((0,), (0,)), ((), ())), preferred_element_type=F32)


def _split3(x):
    hi = _bf(x)
    r = x - _f32(hi)
    mid = _bf(r)
    lo = _bf(r - _f32(mid))
    return hi, mid, lo


def _sigmoid(x):
    return 1.0 / (1.0 + jnp.exp2(x * (-math.log2(math.e))))


def _softplus(x):
    return jnp.maximum(x, 0.0) + jnp.log1p(jnp.exp(-jnp.abs(x)))


def _rmsnorm(x, g):
    return x * lax.rsqrt(jnp.mean(x * x, axis=-1, keepdims=True) + EPS) * g


def _iota(shape, dim):
    return lax.broadcasted_iota(jnp.int32, shape, dim)


def _seg_masks(n, seg, reps=1):
    row = _iota((n, reps * n), 0)
    col = _iota((n, reps * n), 1) % n
    same = True if seg == n else (row // seg) == (col // seg)
    lower = jnp.logical_and(col <= row, same)
    strict = jnp.logical_and(col < row, same)
    upper = jnp.logical_and(col >= row, same)
    return lower, strict, upper, row == col


def _one_hot_bf(mask):
    return jnp.where(mask, 1.0, 0.0).astype(BF16)


def _rr(*gens):
    active = list(gens)
    while active:
        for g in list(active):
            try:
                next(g)
            except StopIteration:
                active.remove(g)
        yield


def _run(*gens):
    for _ in _rr(*gens):
        pass


def _in_kernel(x_ref, g_ref, wa_ref, wb_ref, wc_ref, ws_ref, om_ref, os_ref, h_ref, *, starts):
    j = pl.program_id(1)

    @pl.when(j == 0)
    def _():
        h_ref[...] = _bf(_rmsnorm(x_ref[...], g_ref[...]))
        os_ref[...] = _mm(h_ref[...], ws_ref[...])

    @pl.when(j < starts[1])
    def _():
        om_ref[...] = _bf(_mm(h_ref[...], wa_ref[...]))

    @pl.when(jnp.logical_and(j >= starts[1], j < starts[2]))
    def _():
        om_ref[...] = _bf(_mm(h_ref[...], wb_ref[...]))

    @pl.when(j >= starts[2])
    def _():
        om_ref[...] = _bf(_mm(h_ref[...], wc_ref[...]))


def _in_proj(x, g, w_parts, w_small, layer):
    m = x.shape[0]
    tm = min(m, 2048)
    tn = 1024
    starts = (0, COL_OA // tn, COL_ZB // tn)
    ends = (starts[1], starts[2], N_MAIN // tn)
    wspec = lambda k: pl.BlockSpec(
        (None, D_MODEL, tn), lambda i, j: (layer, 0, jnp.clip(j, starts[k], ends[k] - 1) - starts[k]))
    return pl.pallas_call(
        functools.partial(_in_kernel, starts=starts),
        grid=(m // tm, N_MAIN // tn),
        in_specs=[
            pl.BlockSpec((tm, D_MODEL), lambda i, j: (i, 0)),
            pl.BlockSpec((None, 1, D_MODEL), lambda i, j: (layer, 0, 0)),
            wspec(0), wspec(1), wspec(2),
            pl.BlockSpec((None, D_MODEL, LANES), lambda i, j: (layer, 0, 0)),
        ],
        out_specs=[
            pl.BlockSpec((tm, tn), lambda i, j: (i, j)),
            pl.BlockSpec((tm, LANES), lambda i, j: (i, 0)),
        ],
        out_shape=[jax.ShapeDtypeStruct((m, N_MAIN), BF16), jax.ShapeDtypeStruct((m, LANES), F32)],
        scratch_shapes=[pltpu.VMEM((tm, D_MODEL), BF16)],
        compiler_params=pltpu.CompilerParams(
            dimension_semantics=("parallel", "arbitrary"), vmem_limit_bytes=VMEM_LIMIT),
        name="in_proj",
    )(x, g, *w_parts, w_small)


def _post_kernel(x_ref, ha_ref, hb_ref, ga_ref, gb_ref, p_ref, wpa_ref, wpb_ref, wout_ref, gffn_ref,
                 wup_ref, wdown_ref, gple_ref, wpg_ref, wpp_ref, gfin_ref, o_ref, *, final):
    x = x_ref[...]
    a = _mm(ha_ref[...], wpa_ref[...])
    b = _mm(hb_ref[...], wpb_ref[...])
    mix = _sigmoid(_f32(ga_ref[...])) * a + _sigmoid(_f32(gb_ref[...])) * b
    x = x + _mm(_bf(mix), wout_ref[...])
    hn = _bf(_rmsnorm(x, gffn_ref[...]))
    acc = jnp.zeros_like(x)
    fc = 1024
    for c in range(D_FF // fc):
        u = _mm(hn, wup_ref[:, c * fc:(c + 1) * fc])
        r = jnp.maximum(u, 0.0)
        acc = acc + _mm(_bf(r * r), wdown_ref[c * fc:(c + 1) * fc, :])
    x = x + acc
    hp = _bf(_rmsnorm(x, gple_ref[...]))
    gate = _sigmoid(_mm(hp, wpg_ref[...]))
    x = x + gate * _mm(_bf(p_ref[...]), wpp_ref[...])
    if final:
        x = _rmsnorm(x, gfin_ref[...])
    o_ref[...] = x


def _post(x, ha, hb, proj_main, p, w, layer, final):
    m = x.shape[0]
    tm = min(m, 512)
    row = lambda i: (i, 0)
    wspec = lambda r, c: pl.BlockSpec((None, r, c), lambda i: (layer, 0, 0), pipeline_mode=pl.Buffered(1))
    return pl.pallas_call(
        functools.partial(_post_kernel, final=final),
        grid=(m // tm,),
        in_specs=[
            pl.BlockSpec((tm, D_MODEL), row),
            pl.BlockSpec((tm, W_A_V), row),
            pl.BlockSpec((tm, W_B_V), row),
            pl.BlockSpec((tm, D_MODEL), lambda i: (i, COL_GA // D_MODEL)),
            pl.BlockSpec((tm, D_MODEL), lambda i: (i, COL_GB // D_MODEL)),
            pl.BlockSpec((None, tm, PLE_DIM), lambda i: (layer, i, 0)),
            wspec(W_A_V, D_MODEL),
            wspec(W_B_V, D_MODEL),
            wspec(D_MODEL, D_MODEL),
            wspec(1, D_MODEL),
            wspec(D_MODEL, D_FF),
            wspec(D_FF, D_MODEL),
            wspec(1, D_MODEL),
            wspec(D_MODEL, D_MODEL),
            wspec(PLE_DIM, D_MODEL),
            pl.BlockSpec((1, D_MODEL), lambda i: (0, 0), pipeline_mode=pl.Buffered(1)),
        ],
        out_specs=pl.BlockSpec((tm, D_MODEL), row),
        out_shape=jax.ShapeDtypeStruct((m, D_MODEL), F32),
        compiler_params=pltpu.CompilerParams(
            dimension_semantics=("parallel",), vmem_limit_bytes=VMEM_LIMIT),
        name="post_final" if final else "post",
    )(x, ha, hb, proj_main, proj_main, p, w["w_pa"], w["w_pb"], w["w_out"], w["g_ffn"], w["w_up"],
      w["w_down"], w["g_ple"], w["w_pg"], w["w_pp"], w["g_final"])


def _gate_tables(small, prm, kind, lower_bf, upper_bf, eye_bf):
    n = small.shape[0]
    lane = _iota((n, LANES), 1)
    p = small + prm[0:1, :]
    if kind == "mlstm":
        lf = -_softplus(-p)
        pm = jnp.where(lane < SM_F, p, jnp.where(lane < SM_A, lf, 0.0))
    else:
        g = -jnp.exp(prm[1:2, :]) * _softplus(p)
        beta = _sigmoid(p)
        in_a = jnp.logical_and(lane >= SM_A, lane < SM_BETA)
        in_b = jnp.logical_and(lane >= SM_BETA, lane < SM_BETA + H_B)
        pm = jnp.where(in_a, g, jnp.where(in_b, beta, 0.0))
    parts = _split3(pm)
    cs = sum(_mm(lower_bf, q) for q in parts)
    pm_t = sum(_mm_tn(q, eye_bf) for q in parts) if eye_bf is not None else None
    cs_t = sum(_mm_tn(q, upper_bf) for q in parts)
    return pm, cs, pm_t, cs_t


def _mlstm_out(num, nq, m_row, gna, o):
    den = jnp.maximum(jnp.abs(nq), jnp.exp(-m_row))
    hh = num / den
    hr = hh * lax.rsqrt(jnp.mean(hh * hh, axis=1, keepdims=True) + EPS)
    return hr * gna * _sigmoid(o)


def _mlstm_tile_gen(q_ref, k_ref, v_ref, o_ref, sm_ref, gna_ref, ha_ref, prm, ch, nh, mh, n_chunks, rows_of):
    L = CHUNK
    hs = range(H_A)
    lower, _, upper, diag = _seg_masks(L, L)
    lower_bf, upper_bf, eye_bf = _one_hot_bf(lower), _one_hot_bf(upper), _one_hot_bf(diag)
    scale = DK_A ** -0.5
    for c in range(n_chunks):
        rows = rows_of(c)
        pm, cs, pm_t, cs_t = _gate_tables(sm_ref[rows, :], prm, "mlstm", lower_bf, upper_bf, eye_bf)
        qb = [q_ref[rows, h * DK_A:(h + 1) * DK_A] for h in hs]
        kb = [k_ref[rows, h * DK_A:(h + 1) * DK_A] for h in hs]
        vb = [v_ref[rows, h * DV_A:(h + 1) * DV_A] for h in hs]
        qk = [_mm_nt(qb[h], kb[h]) for h in hs]
        yield
        b_row, a_row, s_loc, kw_loc = [], [], [], []
        for h in hs:
            i_row = pm[:, SM_I + h:SM_I + h + 1]
            b_row.append(cs[:, SM_F + h:SM_F + h + 1])
            dlog = jnp.where(
                lower, b_row[h] - cs_t[SM_F + h:SM_F + h + 1, :] + pm_t[SM_I + h:SM_I + h + 1, :], -jnp.inf)
            a_row.append(jnp.max(dlog, axis=1, keepdims=True))
            s_loc.append(qk[h] * (scale * jnp.exp(dlog - a_row[h])))
            b_last = b_row[h][L - 1:L, :]
            kw_loc.append(_f32(kb[h]) * jnp.exp(i_row + b_last - b_row[h] - a_row[h][L - 1:L, :]))
        sv = [_mm(_bf(s_loc[h]), vb[h]) for h in hs]
        kv = [_mm_tn(_bf(kw_loc[h]), vb[h]) for h in hs]
        qc = [_mm(qb[h], _bf(ch[h])) for h in hs]
        yield
        for h in hs:
            m_row = jnp.maximum(b_row[h] + mh[h], a_row[h])
            w_inter = scale * jnp.exp(b_row[h] + mh[h] - m_row)
            f_intra = jnp.exp(a_row[h] - m_row)
            num = w_inter * qc[h] + f_intra * sv[h]
            nq = (w_inter * jnp.sum(_f32(qb[h]) * nh[h], axis=1, keepdims=True)
                  + f_intra * jnp.sum(s_loc[h], axis=1, keepdims=True))
            ha_ref[rows, h * DV_A:(h + 1) * DV_A] = _bf(_mlstm_out(
                num, nq, m_row, gna_ref[:, h * DV_A:(h + 1) * DV_A], _f32(o_ref[rows, h * DV_A:(h + 1) * DV_A])))
            m_new = m_row[L - 1:L, :]
            dec = jnp.exp(b_row[h][L - 1:L, :] + mh[h] - m_new)
            g_loc = jnp.exp(a_row[h][L - 1:L, :] - m_new)
            ch[h] = dec * ch[h] + g_loc * kv[h]
            nh[h] = dec * nh[h] + g_loc * jnp.sum(kw_loc[h], axis=0, keepdims=True)
            mh[h] = m_new
        yield


def _rep_rows(x, seg):
    g = x.shape[0]
    return jnp.concatenate([jnp.broadcast_to(x[j:j + 1, :], (seg, x.shape[1])) for j in range(g)], axis=0)


def _mlstm_sample_kernel(*refs, seg, aliased):
    (q_ref, k_ref, v_ref, o_ref, sm_ref, prm_ref, gna_ref, c0_ref, n0_ref, m0_ref) = refs[:10]
    ha_ref, c_out, n_out, m_out = refs[10 + aliased:]
    n = q_ref.shape[0]
    nb = n // seg
    hs, js = range(H_A), range(nb)
    sl = [slice(j * seg, (j + 1) * seg) for j in js]
    last = [slice((j + 1) * seg - 1, (j + 1) * seg) for j in js]
    lower, _, upper, diag = _seg_masks(n, seg)
    lower_bf, upper_bf, eye_bf = _one_hot_bf(lower), _one_hot_bf(upper), _one_hot_bf(diag)
    scale = DK_A ** -0.5
    pm, cs, pm_t, cs_t = _gate_tables(sm_ref[...], prm_ref[...], "mlstm", lower_bf, upper_bf, eye_bf)
    m_rep = _rep_rows(m0_ref[...], seg)
    qb = [q_ref[:, h * DK_A:(h + 1) * DK_A] for h in hs]
    kb = [k_ref[:, h * DK_A:(h + 1) * DK_A] for h in hs]
    vb = [v_ref[:, h * DV_A:(h + 1) * DV_A] for h in hs]
    i_row = [pm[:, SM_I + h:SM_I + h + 1] for h in hs]
    b_row = [cs[:, SM_F + h:SM_F + h + 1] for h in hs]
    qk = [_mm_nt(qb[h], kb[h]) for h in hs]
    qc = [jnp.concatenate([_mm(qb[h][sl[j], :], _bf(c0_ref[j, h])) for j in js], axis=0) for h in hs]
    m_row, w_inter, s = [], [], []
    for h in hs:
        dlog = jnp.where(lower, b_row[h] - cs_t[SM_F + h:SM_F + h + 1, :] + pm_t[SM_I + h:SM_I + h + 1, :],
                         -jnp.inf)
        inter = b_row[h] + m_rep[:, h:h + 1]
        m_row.append(jnp.maximum(inter, jnp.max(dlog, axis=1, keepdims=True)))
        w_inter.append(scale * jnp.exp(inter - m_row[h]))
        s.append(qk[h] * (scale * jnp.exp(dlog - m_row[h])))
    sv = [_mm(_bf(s[h]), vb[h]) for h in hs]
    for h in hs:
        n_rep = _rep_rows(n0_ref[:, h, :], seg)
        num = w_inter[h] * qc[h] + sv[h]
        nq = (w_inter[h] * jnp.sum(_f32(qb[h]) * n_rep, axis=1, keepdims=True)
              + jnp.sum(s[h], axis=1, keepdims=True))
        ha_ref[:, h * DV_A:(h + 1) * DV_A] = _bf(_mlstm_out(
            num, nq, m_row[h], gna_ref[:, h * DV_A:(h + 1) * DV_A], _f32(o_ref[:, h * DV_A:(h + 1) * DV_A])))
    for h in hs:
        kf = _f32(kb[h])
        for j in js:
            m_new = m_row[h][last[j], :]
            b_last = b_row[h][last[j], :]
            dec = jnp.exp(b_last + m0_ref[j:j + 1, h:h + 1] - m_new)
            kw = kf[sl[j], :] * jnp.exp(i_row[h][sl[j], :] + b_last - b_row[h][sl[j], :] - m_new)
            c_out[j, h] = dec * c0_ref[j, h] + _mm_tn(_bf(kw), vb[h][sl[j], :])
            n_out[j, h:h + 1, :] = dec * n0_ref[j, h:h + 1, :] + jnp.sum(kw, axis=0, keepdims=True)
            m_out[j:j + 1, h:h + 1] = m_new


def _mlstm_sample(proj_main, proj_small, prm, gna, c0, n0, m0, c_stack, layer, batch, seq):
    depth = c0.shape[0]
    nb = CHUNK // seq
    n = nb * seq
    rb = lambda col: (lambda g: (g, col))
    par = lambda g: (layer, 0, 0)
    aliased = c_stack is not None
    in_specs = [
        pl.BlockSpec((n, W_A_QK), rb(COL_QA // W_A_QK)),
        pl.BlockSpec((n, W_A_QK), rb(COL_KA // W_A_QK)),
        pl.BlockSpec((n, W_A_V), rb(COL_VA // W_A_V)),
        pl.BlockSpec((n, W_A_V), rb(COL_OA // W_A_V)),
        pl.BlockSpec((n, LANES), rb(0)),
        pl.BlockSpec((None, SUBLANES, LANES), par),
        pl.BlockSpec((None, 1, W_A_V), par),
        pl.BlockSpec((None, nb, H_A, DK_A, DV_A), lambda g: (layer, g, 0, 0, 0)),
        pl.BlockSpec((None, nb, H_A, DK_A), lambda g: (layer, g, 0, 0)),
        pl.BlockSpec((None, nb, H_A), lambda g: (layer, g, 0)),
    ]
    args = [proj_main, proj_main, proj_main, proj_main, proj_small, prm, gna, c0, n0, m0]
    if aliased:
        in_specs.append(pl.BlockSpec(memory_space=pl.ANY))
        args.append(c_stack)
    return pl.pallas_call(
        functools.partial(_mlstm_sample_kernel, seg=seq, aliased=int(aliased)),
        grid=(batch // nb,),
        in_specs=in_specs,
        out_specs=[
            pl.BlockSpec((n, W_A_V), rb(0)),
            pl.BlockSpec((None, nb, H_A, DK_A, DV_A), lambda g: (layer, g, 0, 0, 0)),
            pl.BlockSpec((nb, H_A, DK_A), lambda g: (g, 0, 0)),
            pl.BlockSpec((nb, H_A), lambda g: (g, 0)),
        ],
        out_shape=[
            jax.ShapeDtypeStruct((batch * seq, W_A_V), BF16),
            jax.ShapeDtypeStruct((depth, batch, H_A, DK_A, DV_A), F32),
            jax.ShapeDtypeStruct((batch, H_A, DK_A), F32),
            jax.ShapeDtypeStruct((batch, H_A), F32),
        ],
        input_output_aliases={10: 1} if aliased else {},
        compiler_params=pltpu.CompilerParams(
            dimension_semantics=("parallel",), vmem_limit_bytes=VMEM_LIMIT),
        name="mlstm_sample",
    )(*args)


def _conv_silu(xp_ref, w_ref, cv_ref, src0, dst0, n_rows):
    cb = 512
    top = src0 + CONV_W - 1 - SUBLANES
    for c0 in range(0, C_CONV, cb):
        cols = slice(c0, c0 + cb)
        xx = xp_ref[top:top + SUBLANES + n_rows, cols]
        acc = w_ref[CONV_W - 1:CONV_W, cols] * xx[SUBLANES:, :]
        for j in range(1, CONV_W):
            acc = acc + w_ref[CONV_W - 1 - j:CONV_W - j, cols] * pltpu.roll(xx, j, axis=0)[SUBLANES:, :]
        cv_ref[dst0:dst0 + n_rows, cols] = acc * _sigmoid(acc)


CONV_HIST = 16


def _shift_select(n_rows):
    shape = ((CONV_W - 1) * n_rows, CONV_HIST + n_rows)
    r, col = _iota(shape, 0), _iota(shape, 1)
    return _one_hot_bf(col == CONV_HIST + r % n_rows - (r // n_rows + 1))


def _conv_silu_mxu(xb_ref, w_ref, cv_ref, top, dst0, n_rows, shift_bf):
    cb = 512
    for c0 in range(0, C_CONV, cb):
        cols = slice(c0, c0 + cb)
        xx = xb_ref[top:top + CONV_HIST + n_rows, cols]
        sh = _mm(shift_bf, xx)
        acc = w_ref[CONV_W - 1:CONV_W, cols] * _f32(xx[CONV_HIST:, :])
        for j in range(1, CONV_W):
            acc = acc + w_ref[CONV_W - 1 - j:CONV_W - j, cols] * sh[(j - 1) * n_rows:j * n_rows, :]
        cv_ref[dst0:dst0 + n_rows, cols] = acc * _sigmoid(acc)


def _bd_rhs(x_f, first):
    return jnp.concatenate([_bf(jnp.where(first, x_f, 0.0)), _bf(jnp.where(first, 0.0, x_f))], axis=0)


def _pair_split(x2, first, want_lhs=True, want_rhs=True):
    hi = _bf(x2)
    hi_f = _f32(hi)
    lo_f = x2 - hi_f
    lhs3 = jnp.concatenate([hi, hi, _bf(lo_f)], axis=1) if want_lhs else None
    rhs3 = None
    if want_rhs:
        bd_hi = _bd_rhs(hi_f, first)
        rhs3 = jnp.concatenate([bd_hi, _bd_rhs(lo_f, first), bd_hi], axis=0)
    return lhs3, rhs3


def _bd2(a, b):
    z = jnp.zeros_like(a)
    return jnp.concatenate([jnp.concatenate([a, z], axis=1), jnp.concatenate([z, b], axis=1)], axis=0)


def _gdn_intra_gen(load_cv, tabs, masks, n_sq, res):
    lower2, strict2, diag2, first = masks
    items = [(b, p) for b in range(len(tabs)) for p in range(H_B // 2)]
    gam_row, decay2, eg, qn, kn, kbv, rhs, kq = {}, {}, {}, {}, {}, {}, {}, {}
    for b, p in items:
        pm, cs, _, cs_t2 = tabs[b]
        hh = (2 * p, 2 * p + 1)
        gam_row[b, p] = [cs[:, SM_A + h:SM_A + h + 1] for h in hh]
        beta_row = [pm[:, SM_BETA + h:SM_BETA + h + 1] for h in hh]
        gam_lane2 = jnp.where(first[0:1, :], cs_t2[SM_A + hh[0]:SM_A + hh[0] + 1, :],
                              cs_t2[SM_A + hh[1]:SM_A + hh[1] + 1, :])
        gam_row2 = jnp.where(first, gam_row[b, p][0], gam_row[b, p][1])
        decay2[b, p] = jnp.exp(jnp.where(lower2, gam_row2 - gam_lane2, -jnp.inf))
        eg[b, p] = [jnp.exp(g) for g in gam_row[b, p]]
        cq2, ck2, vv2 = load_cv(b, p, 0), load_cv(b, p, 1), load_cv(b, p, 2)
        qn[b, p], kn[b, p], kbv[b, p], rhs[b, p] = [], [], [], []
        for e in range(2):
            cq, ck, vv = (x[:, e * DK_B:(e + 1) * DK_B] for x in (cq2, ck2, vv2))
            qn[b, p].append(cq * (lax.rsqrt(jnp.sum(cq * cq, axis=1, keepdims=True) + EPS) * (DK_B ** -0.5)))
            kn[b, p].append(ck * lax.rsqrt(jnp.sum(ck * ck, axis=1, keepdims=True) + EPS))
            kbv[b, p].append(kn[b, p][e] * beta_row[e])
            rhs[b, p].append(jnp.concatenate([vv * beta_row[e], kbv[b, p][e] * eg[b, p][e]], axis=1))
        lhs = jnp.concatenate([_bf(jnp.concatenate(kbv[b, p], axis=1)), _bf(jnp.concatenate(qn[b, p], axis=1))],
                              axis=0)
        kq[b, p] = _mm_nt(lhs, _bd2(_bf(kn[b, p][0]), _bf(kn[b, p][1])))
    yield
    n = lower2.shape[0]
    eye2 = jnp.where(diag2, 1.0, 0.0)
    pw, tinv, l_pw, r_pw = {}, {}, {}, {}
    for it in items:
        pw[it] = jnp.where(strict2, -(kq[it][0:n, :] * decay2[it]), 0.0)
        tinv[it] = eye2 + pw[it]
        l_pw[it], r_pw[it] = _pair_split(pw[it], first)
    for step in range(n_sq):
        for it in items:
            pw[it] = _mm(l_pw[it], r_pw[it])
        yield
        for it in items:
            l_pw[it], r_pw[it] = _pair_split(pw[it], first, want_lhs=step + 1 < n_sq)
            tinv[it] = tinv[it] + _mm(_pair_split(tinv[it], first, want_rhs=False)[0], r_pw[it])
        yield
    sol = {}
    for it in items:
        parts = []
        for e in range(2):
            hi = _bf(rhs[it][e])
            parts.append((hi, _bf(rhs[it][e] - _f32(hi))))
        bd_hi = _bd2(parts[0][0], parts[1][0])
        rhs3 = jnp.concatenate([bd_hi, _bd2(parts[0][1], parts[1][1]), bd_hi], axis=0)
        sol[it] = _mm(_pair_split(tinv[it], first, want_rhs=False)[0], rhs3)
    yield
    for it in items:
        w_v = [sol[it][:, e * 2 * DV_B:e * 2 * DV_B + DV_B] for e in range(2)]
        w_k = [_bf(sol[it][:, e * 2 * DV_B + DV_B:(e + 1) * 2 * DV_B]) for e in range(2)]
        q_dec = [_bf(qn[it][e] * eg[it][e]) for e in range(2)]
        res[it] = (q_dec, kn[it], w_v, w_k, _bf(kq[it][n:, :] * decay2[it]), gam_row[it])


def _gdn_scan_pair(res_it, s_pair, g_l):
    q_dec, kn, w_v, w_k, qk2, gam_row = res_it
    n = w_v[0].shape[0]
    r = [_mm(jnp.concatenate([w_k[e], q_dec[e]], axis=0), _bf(s_pair[e])) for e in range(2)]
    yield
    u = [_bf(w_v[e] - r[e][0:n, :]) for e in range(2)]
    o2 = jnp.concatenate([r[0][n:, :], r[1][n:, :]], axis=1) + _mm(qk2, _bd2(u[0], u[1]))
    for e in range(2):
        k_dec = _bf(kn[e] * jnp.exp(g_l[e] - gam_row[e]))
        s_pair[e] = s_pair[e] * jnp.exp(g_l[e]) + _mm_tn(k_dec, u[e])
    yield
    return o2


def _gdn_out(o, gnb, z):
    return o * lax.rsqrt(jnp.mean(o * o, axis=1, keepdims=True) + EPS) * gnb * (z * _sigmoid(z))


def _gdn_tile_gen(z_ref, sm_ref, gnb_ref, hb_ref, cv_s, prm, state, n_chunks, io_rows):
    L = CHUNK
    ps = range(H_B // 2)
    rows = [slice(c * L, (c + 1) * L) for c in range(n_chunks)]
    io = [io_rows(c) for c in range(n_chunks)]
    lower, _, _, _ = _seg_masks(L, L)
    lower2, strict2, upper2, diag2 = _seg_masks(L, L, reps=2)
    first = _iota((L, 2 * L), 1) < L
    masks = (lower2, strict2, diag2, first)
    lower_bf, upper2_bf = _one_hot_bf(lower), _one_hot_bf(upper2)
    n_sq = int(math.log2(L)) - 1
    res, outs = {}, {}

    def intra(chunks):
        tabs = [_gate_tables(sm_ref[io[c], :], prm, "gdn", lower_bf, upper2_bf, None) for c in chunks]
        load = lambda b, p, part: _f32(cv_s[rows[chunks[b]],
                                            part * W_B_QK + 2 * p * DK_B:part * W_B_QK + 2 * (p + 1) * DK_B])
        sub = {}
        yield from _gdn_intra_gen(load, tabs, masks, n_sq, sub)
        for (b, p), v in sub.items():
            res[chunks[b], p] = v

    def scan(chunks):
        for c in chunks:
            gens = []
            for p in ps:
                g_l = [g[L - 1:L, :] for g in res[c, p][5]]
                gens.append(_gdn_scan_pair(res[c, p], state[p], g_l))
            for stage in range(2):
                for g in gens:
                    next(g)
                yield
            for p, g in zip(ps, gens):
                try:
                    next(g)
                except StopIteration as done:
                    outs[c, p] = done.value

    def emit(chunks):
        for c in chunks:
            for p in ps:
                for e in range(2):
                    h = 2 * p + e
                    hb_ref[io[c], h * DV_B:(h + 1) * DV_B] = _bf(_gdn_out(
                        outs[c, p][:, e * DV_B:(e + 1) * DV_B], gnb_ref[...],
                        _f32(z_ref[io[c], h * DV_B:(h + 1) * DV_B])))

    half = n_chunks // 2
    first_half, second_half = list(range(half)), list(range(half, n_chunks))
    yield from intra(first_half)
    yield from _rr(intra(second_half), scan(first_half))
    emit(first_half)
    yield from scan(second_half)
    emit(second_half)


def _mlstm_prompt_kernel(q_ref, k_ref, v_ref, o_ref, sm_ref, prm_ref, gna_ref,
                         ha_ref, c_out, n_out, m_out, c_s, n_s, m_s, *, n_chunks):
    t = pl.program_id(1)
    hs = range(H_A)

    @pl.when(t == 0)
    def _():
        c_s[...] = jnp.zeros_like(c_s)
        n_s[...] = jnp.zeros_like(n_s)
        m_s[...] = jnp.zeros_like(m_s)

    prm = prm_ref[...]

    def chunk(c, carry):
        ch = [c_s[h] for h in hs]
        nh = [n_s[h:h + 1, :] for h in hs]
        mh = [m_s[h:h + 1, 0:1] for h in hs]
        rows = pl.ds(pl.multiple_of(c * CHUNK, CHUNK), CHUNK)
        _run(_mlstm_tile_gen(q_ref, k_ref, v_ref, o_ref, sm_ref, gna_ref, ha_ref, prm, ch, nh, mh, 1,
                             lambda _: rows))
        for h in hs:
            c_s[h] = ch[h]
            n_s[h:h + 1, :] = nh[h]
            m_s[h:h + 1, :] = jnp.broadcast_to(mh[h], (1, LANES))
        return carry

    lax.fori_loop(0, n_chunks, chunk, 0)

    @pl.when(t == pl.num_programs(1) - 1)
    def _():
        c_out[0] = c_s[...]
        n_out[0] = n_s[0:H_A, :]
        m_out[0] = m_s[...]


def _mlstm_prompt(proj_main, proj_small, prm, gna, layer, batch, seq):
    tt = min(seq, GRID_T)
    nt = seq // tt
    rb = lambda col: (lambda b, t: (b * nt + t, col))
    par = lambda b, t: (layer, 0, 0)
    return pl.pallas_call(
        functools.partial(_mlstm_prompt_kernel, n_chunks=tt // CHUNK),
        grid=(batch, nt),
        in_specs=[
            pl.BlockSpec((tt, W_A_QK), rb(COL_QA // W_A_QK)),
            pl.BlockSpec((tt, W_A_QK), rb(COL_KA // W_A_QK)),
            pl.BlockSpec((tt, W_A_V), rb(COL_VA // W_A_V)),
            pl.BlockSpec((tt, W_A_V), rb(COL_OA // W_A_V)),
            pl.BlockSpec((tt, LANES), rb(0)),
            pl.BlockSpec((None, SUBLANES, LANES), par),
            pl.BlockSpec((None, 1, W_A_V), par),
        ],
        out_specs=[
            pl.BlockSpec((tt, W_A_V), rb(0)),
            pl.BlockSpec((1, H_A, DK_A, DV_A), lambda b, t: (b, 0, 0, 0)),
            pl.BlockSpec((1, H_A, DK_A), lambda b, t: (b, 0, 0)),
            pl.BlockSpec((1, SUBLANES, LANES), lambda b, t: (b, 0, 0)),
        ],
        out_shape=[
            jax.ShapeDtypeStruct((batch * seq, W_A_V), BF16),
            jax.ShapeDtypeStruct((batch, H_A, DK_A, DV_A), F32),
            jax.ShapeDtypeStruct((batch, H_A, DK_A), F32),
            jax.ShapeDtypeStruct((batch, SUBLANES, LANES), F32),
        ],
        scratch_shapes=[
            pltpu.VMEM((H_A, DK_A, DV_A), F32),
            pltpu.VMEM((SUBLANES, LANES), F32),
            pltpu.VMEM((SUBLANES, LANES), F32),
        ],
        compiler_params=pltpu.CompilerParams(
            dimension_semantics=("parallel", "arbitrary"), vmem_limit_bytes=VMEM_LIMIT),
        name="mlstm_prompt",
    )(proj_main, proj_main, proj_main, proj_main, proj_small, prm, gna)


def _gdn_prompt_kernel(x_ref, z_ref, sm_ref, cw_ref, prm_ref, gnb_ref,
                       hb_ref, s_out, conv_out, s_s, xp_s, cv_s, *, n_tiles, n_chunks):
    t = pl.program_id(1)
    L = CHUNK
    tt = n_chunks * L
    ps = range(H_B // 2)

    @pl.when(t == 0)
    def _():
        s_s[...] = jnp.zeros_like(s_s)
        xp_s[0:CONV_HIST, :] = jnp.zeros((CONV_HIST, C_CONV), BF16)

    prm = prm_ref[...]
    shift_bf = _shift_select(L)

    def tile(ti, carry):
        base = pl.multiple_of(ti * tt, tt)
        xp_s[CONV_HIST:CONV_HIST + tt, :] = x_ref[pl.ds(base, tt), :]
        for c in range(n_chunks):
            _conv_silu_mxu(xp_s, cw_ref, cv_s, c * L, c * L, L, shift_bf)
        state = [[s_s[2 * p], s_s[2 * p + 1]] for p in ps]
        _run(_gdn_tile_gen(z_ref, sm_ref, gnb_ref, hb_ref, cv_s, prm, state, n_chunks,
                           lambda c: pl.ds(pl.multiple_of(base + c * L, L), L)))
        for p in ps:
            s_s[2 * p] = state[p][0]
            s_s[2 * p + 1] = state[p][1]
        xp_s[0:CONV_HIST, :] = xp_s[tt:tt + CONV_HIST, :]
        return carry

    lax.fori_loop(0, n_tiles, tile, 0)

    @pl.when(t == pl.num_programs(1) - 1)
    def _():
        s_out[0] = s_s[...]
        conv_out[0] = _f32(xp_s[0:CONV_HIST, :])[CONV_HIST - (CONV_W - 1):, :]


def _gdn_prompt(proj_main, proj_small, conv_w, prm, gnb, layer, batch, seq):
    tt = TILE_T
    blk = min(seq, GRID_T)
    nt = seq // blk
    rb = lambda col: (lambda b, t: (b * nt + t, col))
    par = lambda b, t: (layer, 0, 0)
    return pl.pallas_call(
        functools.partial(_gdn_prompt_kernel, n_tiles=blk // tt, n_chunks=tt // CHUNK),
        grid=(batch, nt),
        in_specs=[
            pl.BlockSpec((blk, C_CONV), rb(COL_QKVB // C_CONV)),
            pl.BlockSpec((blk, W_B_V), rb(COL_ZB // W_B_V)),
            pl.BlockSpec((blk, LANES), rb(0)),
            pl.BlockSpec((None, CONV_W, C_CONV), par),
            pl.BlockSpec((None, SUBLANES, LANES), par),
            pl.BlockSpec((None, 1, DV_B), par),
        ],
        out_specs=[
            pl.BlockSpec((blk, W_B_V), rb(0)),
            pl.BlockSpec((1, H_B, DK_B, DV_B), lambda b, t: (b, 0, 0, 0)),
            pl.BlockSpec((1, CONV_W - 1, C_CONV), lambda b, t: (b, 0, 0)),
        ],
        out_shape=[
            jax.ShapeDtypeStruct((batch * seq, W_B_V), BF16),
            jax.ShapeDtypeStruct((batch, H_B, DK_B, DV_B), F32),
            jax.ShapeDtypeStruct((batch, CONV_W - 1, C_CONV), F32),
        ],
        scratch_shapes=[
            pltpu.VMEM((H_B, DK_B, DV_B), F32),
            pltpu.VMEM((tt + CONV_HIST, C_CONV), BF16),
            pltpu.VMEM((tt, C_CONV), F32),
        ],
        compiler_params=pltpu.CompilerParams(
            dimension_semantics=("parallel", "arbitrary"), vmem_limit_bytes=VMEM_LIMIT),
        name="gdn_prompt",
    )(proj_main, proj_main, proj_small, conv_w, prm, gnb)


def _gdn_sample_kernel(*refs, seg, aliased):
    x_ref, z_ref, sm_ref, cw_ref, prm_ref, gnb_ref, s0_ref, cb_ref = refs[:8]
    hb_ref, s_out, conv_out, xp_s, cv_s = refs[8 + aliased:]
    n = x_ref.shape[0]
    nb = n // seg
    ps, js = range(H_B // 2), range(nb)
    sl = [slice(j * seg, (j + 1) * seg) for j in js]
    stride = 2 * SUBLANES
    for j in js:
        base = j * stride + SUBLANES
        xp_s[base - SUBLANES:base, :] = jnp.zeros((SUBLANES, C_CONV), F32)
        xp_s[base - (CONV_W - 1):base, :] = cb_ref[j]
        xp_s[base:base + seg, :] = _f32(x_ref[sl[j], :])
        _conv_silu(xp_s, cw_ref, cv_s, base - (CONV_W - 1), j * seg, seg)
        conv_out[j] = xp_s[base + seg - (CONV_W - 1):base + seg, :]

    lower, _, _, _ = _seg_masks(n, seg)
    lower2, strict2, upper2, diag2 = _seg_masks(n, seg, reps=2)
    first = _iota((n, 2 * n), 1) < n
    masks = (lower2, strict2, diag2, first)
    n_sq = int(math.log2(seg)) - 1
    tabs = [_gate_tables(sm_ref[...], prm_ref[...], "gdn", _one_hot_bf(lower), _one_hot_bf(upper2), None)]
    load = lambda b, p, part: cv_s[:, part * W_B_QK + 2 * p * DK_B:part * W_B_QK + 2 * (p + 1) * DK_B]
    res = {}
    _run(_gdn_intra_gen(load, tabs, masks, n_sq, res))
    r, u = {}, {}
    for p in ps:
        q_dec, kn, w_v, w_k, qk2, gam_row = res[0, p]
        for e in range(2):
            h = 2 * p + e
            lhs = [jnp.concatenate([w_k[e][sl[j], :], q_dec[e][sl[j], :]], axis=0) for j in js]
            r[h] = [_mm(lhs[j], _bf(s0_ref[j, h])) for j in js]
    for p in ps:
        q_dec, kn, w_v, w_k, qk2, gam_row = res[0, p]
        for e in range(2):
            h = 2 * p + e
            u[h] = _bf(w_v[e] - jnp.concatenate([r[h][j][0:seg, :] for j in js], axis=0))
        oq = [jnp.concatenate([r[2 * p + e][j][seg:, :] for j in js], axis=0) for e in range(2)]
        o2 = jnp.concatenate(oq, axis=1) + _mm(qk2, _bd2(u[2 * p], u[2 * p + 1]))
        for e in range(2):
            h = 2 * p + e
            hb_ref[:, h * DV_B:(h + 1) * DV_B] = _bf(_gdn_out(
                o2[:, e * DV_B:(e + 1) * DV_B], gnb_ref[...], _f32(z_ref[:, h * DV_B:(h + 1) * DV_B])))
    for p in ps:
        q_dec, kn, w_v, w_k, qk2, gam_row = res[0, p]
        for e in range(2):
            h = 2 * p + e
            for j in js:
                g_l = gam_row[e][(j + 1) * seg - 1:(j + 1) * seg, :]
                k_dec = kn[e][sl[j], :] * jnp.exp(g_l - gam_row[e][sl[j], :])
                s_out[j, h] = s0_ref[j, h] * jnp.exp(g_l) + _mm_tn(_bf(k_dec), u[h][sl[j], :])


def _gdn_sample(proj_main, proj_small, conv_w, prm, gnb, s0, cb, s_stack, layer, batch, seq):
    depth = s0.shape[0]
    nb = CHUNK // seq
    n = nb * seq
    rb = lambda col: (lambda g: (g, col))
    par = lambda g: (layer, 0, 0)
    aliased = s_stack is not None
    in_specs = [
        pl.BlockSpec((n, C_CONV), rb(COL_QKVB // C_CONV)),
        pl.BlockSpec((n, W_B_V), rb(COL_ZB // W_B_V)),
        pl.BlockSpec((n, LANES), rb(0)),
        pl.BlockSpec((None, CONV_W, C_CONV), par),
        pl.BlockSpec((None, SUBLANES, LANES), par),
        pl.BlockSpec((None, 1, DV_B), par),
        pl.BlockSpec((None, nb, H_B, DK_B, DV_B), lambda g: (layer, g, 0, 0, 0)),
        pl.BlockSpec((None, nb, CONV_W - 1, C_CONV), lambda g: (layer, g, 0, 0)),
    ]
    args = [proj_main, proj_main, proj_small, conv_w, prm, gnb, s0, cb]
    if aliased:
        in_specs.append(pl.BlockSpec(memory_space=pl.ANY))
        args.append(s_stack)
    return pl.pallas_call(
        functools.partial(_gdn_sample_kernel, seg=seq, aliased=int(aliased)),
        grid=(batch // nb,),
        in_specs=in_specs,
        out_specs=[
            pl.BlockSpec((n, W_B_V), rb(0)),
            pl.BlockSpec((None, nb, H_B, DK_B, DV_B), lambda g: (layer, g, 0, 0, 0)),
            pl.BlockSpec((nb, CONV_W - 1, C_CONV), lambda g: (g, 0, 0)),
        ],
        out_shape=[
            jax.ShapeDtypeStruct((batch * seq, W_B_V), BF16),
            jax.ShapeDtypeStruct((depth, batch, H_B, DK_B, DV_B), F32),
            jax.ShapeDtypeStruct((batch, CONV_W - 1, C_CONV), F32),
        ],
        scratch_shapes=[
            pltpu.VMEM((nb * 2 * SUBLANES, C_CONV), F32),
            pltpu.VMEM((n, C_CONV), F32),
        ],
        input_output_aliases={8: 1} if aliased else {},
        compiler_params=pltpu.CompilerParams(
            dimension_semantics=("parallel",), vmem_limit_bytes=VMEM_LIMIT),
        name="gdn_sample",
    )(*args)


def _pack_w_in(w_in):
    w = w_in.astype(BF16)
    o = 0
    parts = {}
    for name, width in (("qa", W_A_QK), ("ka", W_A_QK), ("va", W_A_V), ("ip", H_A), ("fp", H_A), ("oa", W_A_V),
                        ("qkvb", C_CONV), ("ab", H_B), ("betab", H_B), ("zb", W_B_V), ("ga", D_MODEL),
                        ("gb", D_MODEL)):
        parts[name] = w[:, :, o:o + width]
        o += width
    o_qkv = w[:, :, COL_OA + 2 * H_A:COL_ZB + 2 * H_A]
    z_gates = w[:, :, COL_ZB + 2 * H_A + 2 * H_B:]
    assert o_qkv.shape[-1] == COL_ZB - COL_OA and z_gates.shape[-1] == N_MAIN - COL_ZB
    small = jnp.concatenate([parts[k] for k in ("ip", "fp", "ab", "betab")], axis=-1)
    small = jnp.pad(small, ((0, 0), (0, 0), (0, LANES - small.shape[-1])))
    return (w, o_qkv, z_gates), small


def _param_rows(b_igate, b_fgate, dt_bias, a_log):
    bias = jnp.concatenate([b_igate, b_fgate, dt_bias], axis=-1)
    bias = jnp.pad(bias, ((0, 0), (0, LANES - bias.shape[-1])))
    alog = jnp.pad(a_log, ((0, 0), (SM_A, LANES - SM_A - H_B)))
    rows = jnp.stack([bias, alog], axis=1)
    return jnp.pad(rows, ((0, 0), (0, SUBLANES - 2), (0, 0))).astype(F32)


def kernel(x_prompt, x_sample, p_prompt, p_sample, state_mlstm_C, state_mlstm_n, state_mlstm_m, state_gdn_S,
           state_gdn_conv, g_mix, w_in, b_igate, b_fgate, g_norm_a, conv_w, a_log, dt_bias, g_norm_b, w_pa, w_pb,
           w_out, g_ffn, w_up, w_down, g_ple, w_pg, w_pp, g_final):
    depth = w_in.shape[0]
    bp, tp, _ = x_prompt.shape
    bs, ts, _ = x_sample.shape
    w_main, w_small = _pack_w_in(w_in)
    prm = _param_rows(b_igate, b_fgate, dt_bias, a_log)
    w = {k: v.astype(BF16) for k, v in (("w_pa", w_pa), ("w_pb", w_pb), ("w_out", w_out), ("w_up", w_up),
                                          ("w_down", w_down), ("w_pg", w_pg), ("w_pp", w_pp))}
    w["g_ffn"] = g_ffn.reshape(depth, 1, D_MODEL)
    w["g_ple"] = g_ple.reshape(depth, 1, D_MODEL)
    w["g_final"] = g_final.reshape(1, D_MODEL)
    g_in = g_mix.reshape(depth, 1, D_MODEL)
    gna = g_norm_a.reshape(depth, 1, W_A_V)
    gnb = g_norm_b.reshape(depth, 1, DV_B)
    pp = p_prompt.reshape(depth, bp * tp, PLE_DIM)
    ps = p_sample.reshape(depth, bs * ts, PLE_DIM)
    xp = x_prompt.reshape(bp * tp, D_MODEL)
    xs = x_sample.reshape(bs * ts, D_MODEL)
    outs = {k: [] for k in ("pC", "pn", "pm", "pS", "pc", "sn", "sm", "sc")}
    c_stack = s_stack = None
    for i in range(depth):
        final = i == depth - 1

        pj, pj_s = _in_proj(xp, g_in, w_main, w_small, i)
        ha, c1, n1, m1 = _mlstm_prompt(pj, pj_s, prm, gna, i, bp, tp)
        hb, s1, v1 = _gdn_prompt(pj, pj_s, conv_w, prm, gnb, i, bp, tp)
        xp = _post(xp, ha, hb, pj, pp, w, i, final)
        outs["pC"].append(c1)
        outs["pn"].append(n1)
        outs["pm"].append(m1[:, :H_A, 0])
        outs["pS"].append(s1)
        outs["pc"].append(v1)

        qj, qj_s = _in_proj(xs, g_in, w_main, w_small, i)
        ha, c_stack, n2, m2 = _mlstm_sample(qj, qj_s, prm, gna, state_mlstm_C, state_mlstm_n, state_mlstm_m,
                                            c_stack, i, bs, ts)
        hb, s_stack, v2 = _gdn_sample(qj, qj_s, conv_w, prm, gnb, state_gdn_S, state_gdn_conv, s_stack,
                                      i, bs, ts)
        xs = _post(xs, ha, hb, qj, ps, w, i, final)
        outs["sn"].append(n2)
        outs["sm"].append(m2)
        outs["sc"].append(v2)

    st = {k: jnp.stack(v) for k, v in outs.items()}
    return (xp.reshape(bp, tp, D_MODEL), xs.reshape(bs, ts, D_MODEL),
            st["pC"], st["pn"], st["pm"], st["pS"], st["pc"],
            c_stack, st["sn"], st["sm"], s_stack, st["sc"])
```
